```python
import jax
import jax.numpy as jnp
from jax import lax
import numpy as np

D_MODEL = 1024
BATCH = 32
SEQ = 2048
DEPTH = 1
DEC_BATCH = 128
DEC_SEQ = 4
PAST_LEN = 16384
PAGE_SIZE = 128

N_META = 16
MLA_HEADS = 8
Q_LORA = 384
KV_LORA = 256
NOPE_DIM = 64
ROPE_DIM = 32
V_DIM = 64
QK_DIM = NOPE_DIM + ROPE_DIM
ROPE_THETA = 10000.0
MLA_SCALE = QK_DIM ** -0.5
FOX_HEADS = 8
FOX_DIM = 64
FOX_WIDTH = FOX_HEADS * FOX_DIM
FOX_SCALE = FOX_DIM ** -0.5
FORGET_BIAS_INIT = 2.0
N_EXPERTS = 32
TOP_K = 4
D_FF = 1024
SWIGLU_ALPHA = 1.702
SWIGLU_LIMIT = 7.0
MOE_BLOCK = 256
Q_BLOCK = 128
NORM_EPS = 1e-6
POOL_NUM = 5
POOL_DEN = 4

IN_SIZES = (Q_LORA, KV_LORA, ROPE_DIM, FOX_WIDTH, FOX_WIDTH, FOX_WIDTH, FOX_HEADS, D_MODEL, D_MODEL)
IN_COLS = sum(IN_SIZES)
IN_SPLITS = tuple(int(s) for s in np.cumsum(IN_SIZES)[:-1])

kernel_name = 'mla_fox_gated_moe_step'


def rmsnorm(x, g):
    xf = x.astype(jnp.float32)
    y = xf * lax.rsqrt(jnp.mean(xf * xf, axis=-1, keepdims=True) + NORM_EPS)
    return (y * g.astype(jnp.float32)).astype(x.dtype)


def rope(x, pos):
    half = ROPE_DIM // 2
    inv_freq = ROPE_THETA ** (-jnp.arange(half, dtype=jnp.float32) / half)
    ang = pos.astype(jnp.float32)[:, None] * inv_freq[None, :]
    cos = jnp.cos(ang)[:, None, :]
    sin = jnp.sin(ang)[:, None, :]
    xf = x.astype(jnp.float32)
    x1, x2 = xf[..., :half], xf[..., half:]
    return jnp.concatenate([x1 * cos - x2 * sin, x2 * cos + x1 * sin], axis=-1).astype(x.dtype)


def input_projection(hn, w_in):
    proj = jnp.einsum('btd,dc->btc', hn, w_in)
    return jnp.split(proj, IN_SPLITS, axis=-1)


def mla_queries(q_lat, pos, p):
    q = jnp.einsum('btr,rhd->bthd', rmsnorm(q_lat, p['g_qlat']), p['w_uq'])
    q = rmsnorm(q, p['g_mla_q'])
    return jnp.concatenate([q[..., :NOPE_DIM], rope(q[..., NOPE_DIM:], pos)], axis=-1)


def mla_keys(c_kv, k_rope, pos, p):
    kv = jnp.einsum('btc,chd->bthd', c_kv, p['w_ukv'])
    k_nope, v = kv[..., :NOPE_DIM], kv[..., NOPE_DIM:]
    kr = jnp.broadcast_to(k_rope[:, :, None, :], k_nope.shape[:-1] + (ROPE_DIM,))
    k = rmsnorm(jnp.concatenate([k_nope, kr], axis=-1), p['g_mla_k'])
    k = jnp.concatenate([k[..., :NOPE_DIM], rope(k[..., NOPE_DIM:], pos)], axis=-1)
    return k, v


def fox_projections(fq, fk, fv, f_logit, p):
    B, T, _ = fq.shape
    q = rmsnorm(fq.reshape(B, T, FOX_HEADS, FOX_DIM), p['g_fox_q'])
    k = rmsnorm(fk.reshape(B, T, FOX_HEADS, FOX_DIM), p['g_fox_k'])
    v = fv.reshape(B, T, FOX_HEADS, FOX_DIM)
    logf = jax.nn.log_sigmoid((f_logit + p['b_forget']).astype(jnp.float32))
    return q, k, v, logf


def forget_bias(q_F, k_F):
    return jnp.swapaxes(q_F, 1, 2)[..., :, None] - jnp.swapaxes(k_F, 1, 2)[..., None, :]


def attend(q, q_pos, k, v, k_pos, scale, q_F=None, k_F=None):
    s = jnp.einsum('bqhd,bkhd->bhqk', q, k, preferred_element_type=jnp.float32) * scale
    if q_F is not None:
        s = s + forget_bias(q_F, k_F)
    s = jnp.where(k_pos[None, :] <= q_pos[:, None], s, -jnp.inf)
    pr = jax.nn.softmax(s, axis=-1)
    return jnp.einsum('bhqk,bkhd->bqhd', pr.astype(v.dtype), v)


def block_sweep(q, q_pos, k, v, k_pos, scale, q_F=None, k_F=None):
    B, Lq, H, d = q.shape
    nb = Lq // Q_BLOCK
    qb = jnp.moveaxis(q.reshape(B, nb, Q_BLOCK, H, d), 1, 0)
    pb = q_pos.reshape(nb, Q_BLOCK)
    if q_F is None:
        out = lax.map(lambda a: attend(a[0], a[1], k, v, k_pos, scale), (qb, pb))
    else:
        fb = jnp.moveaxis(q_F.reshape(B, nb, Q_BLOCK, H), 1, 0)
        out = lax.map(lambda a: attend(a[0], a[1], k, v, k_pos, scale, a[2], k_F), (qb, pb, fb))
    return jnp.moveaxis(out, 0, 1).reshape(B, Lq, H, v.shape[-1])


def online_update(carry, s, v):
    m, l, acc = carry
    m_new = jnp.maximum(m, jnp.max(s, axis=-1))
    corr = jnp.exp(m - m_new)
    pr = jnp.exp(s - m_new[..., None])
    l = l * corr + jnp.sum(pr, axis=-1)
    acc = acc * corr[..., None] + jnp.einsum('bhqk,bkhd->bhqd', pr, v.astype(jnp.float32))
    return (m_new, l, acc)


def paged_attention(q, q_pos, scale, page_kv, n_pages, k_new, v_new, q_F=None, k_F_new=None):
    Bd, S, H, _ = q.shape

    def scores(k, k_F):
        s = jnp.einsum('bqhd,bkhd->bhqk', q, k, preferred_element_type=jnp.float32) * scale
        if q_F is not None:
            s = s + forget_bias(q_F, k_F)
        return s

    def step(carry, j):
        k, v, k_F = page_kv(j)
        return online_update(carry, scores(k, k_F), v), None

    init = (jnp.full((Bd, H, S), -jnp.inf, jnp.float32),
            jnp.zeros((Bd, H, S), jnp.float32),
            jnp.zeros((Bd, H, S, v_new.shape[-1]), jnp.float32))
    carry, _ = lax.scan(step, init, jnp.arange(n_pages, dtype=jnp.int32))
    s_new = jnp.where(q_pos[None, :] <= q_pos[:, None], scores(k_new, k_F_new), -jnp.inf)
    m, l, acc = online_update(carry, s_new, v_new)
    return jnp.swapaxes(acc / l[..., None], 1, 2).astype(q.dtype)


def merge_branches(o_mla, o_fox, gate_a, gate_b, p):
    B, T = o_mla.shape[:2]
    u_a = jnp.einsum('bte,ed->btd', o_mla.reshape(B, T, MLA_HEADS * V_DIM), p['w_branch_mla'])
    u_b = jnp.einsum('bte,ed->btd', o_fox.reshape(B, T, FOX_WIDTH), p['w_branch_fox'])
    mixed = jax.nn.sigmoid(gate_a) * u_a + jax.nn.sigmoid(gate_b) * u_b
    return jnp.einsum('btd,de->bte', mixed, p['w_out'])


def expert_ffn(xb, e, p):
    g = xb @ p['w_gate'][e] + p['b_gate'][e]
    u = xb @ p['w_up'][e] + p['b_up'][e]
    g = jnp.minimum(g, SWIGLU_LIMIT)
    u = jnp.clip(u, -SWIGLU_LIMIT, SWIGLU_LIMIT)
    hid = (u + 1.0) * g * jax.nn.sigmoid(SWIGLU_ALPHA * g)
    return hid @ p['w_down'][e] + p['b_down'][e]


def moe(x, p):
    B, T, D = x.shape
    n_tok = B * T
    xt = x.reshape(n_tok, D)
    logits = jnp.einsum('td,de->te', xt, p['w_router'], preferred_element_type=jnp.float32) + p['b_router'].astype(jnp.float32)
    top_val, top_idx = lax.top_k(logits, TOP_K)
    gates = jax.nn.softmax(top_val, axis=-1)
    flat_e = top_idx.reshape(-1)
    order = jnp.argsort(flat_e)
    sorted_e = flat_e[order]
    tok = order // TOP_K
    counts = jnp.bincount(flat_e, length=N_EXPERTS)
    padded = (counts + MOE_BLOCK - 1) // MOE_BLOCK * MOE_BLOCK
    pad_end = jnp.cumsum(padded)
    pad_start = pad_end - padded
    start = jnp.cumsum(counts) - counts
    n_assign = n_tok * TOP_K
    dest = pad_start[sorted_e] + (jnp.arange(n_assign, dtype=jnp.int32) - start[sorted_e])
    n_blocks = -(-n_assign // MOE_BLOCK) + N_EXPERTS
    rows = jnp.zeros((n_blocks * MOE_BLOCK, D), x.dtype).at[dest].set(xt[tok])
    block_e = jnp.minimum(jnp.searchsorted(pad_end, jnp.arange(n_blocks, dtype=jnp.int32) * MOE_BLOCK, side='right'), N_EXPERTS - 1)
    out = lax.map(lambda a: expert_ffn(a[0], a[1], p), (rows.reshape(n_blocks, MOE_BLOCK, D), block_e))
    y_assign = out.reshape(n_blocks * MOE_BLOCK, D)[dest]
    w = gates.reshape(-1)[order].astype(x.dtype)
    y = jax.ops.segment_sum(w[:, None] * y_assign, tok, num_segments=n_tok)
    return y.reshape(B, T, D)


def prompt_layer(h, p, last):
    B, L, _ = h.shape
    pos = jnp.arange(L, dtype=jnp.int32)
    hn = rmsnorm(h, p['attn_norm'])
    q_lat, kv_lat, k_rope, fq, fk, fv, f_logit, gate_a, gate_b = input_projection(hn, p['w_in'])
    c_kv = rmsnorm(kv_lat, p['g_kvlat'])
    q_mla = mla_queries(q_lat, pos, p)
    k_mla, v_mla = mla_keys(c_kv, k_rope, pos, p)
    q_fox, k_fox, v_fox, logf = fox_projections(fq, fk, fv, f_logit, p)
    F = jnp.cumsum(logf, axis=1)
    o_mla = block_sweep(q_mla[:, N_META:], pos[N_META:], k_mla, v_mla, pos, MLA_SCALE)
    o_fox = block_sweep(q_fox[:, N_META:], pos[N_META:], k_fox, v_fox, pos, FOX_SCALE, F[:, N_META:], F)
    q0 = N_META
    if not last:
        m_mla = attend(q_mla[:, :N_META], pos[:N_META], k_mla, v_mla, pos, MLA_SCALE)
        m_fox = attend(q_fox[:, :N_META], pos[:N_META], k_fox, v_fox, pos, FOX_SCALE, F[:, :N_META], F)
        o_mla = jnp.concatenate([m_mla, o_mla], axis=1)
        o_fox = jnp.concatenate([m_fox, o_fox], axis=1)
        q0 = 0
    h = h[:, q0:] + merge_branches(o_mla, o_fox, gate_a[:, q0:], gate_b[:, q0:], p)
    h = h + moe(rmsnorm(h, p['ffn_norm']), p)
    return h, (c_kv, k_rope, k_fox, v_fox, logf)


def sample_layer(h, p, li, cache_mla_latent, cache_mla_krope, cache_fox_k, cache_fox_v, cache_fox_logf, page_table):
    Bd, S, _ = h.shape
    n_pages = page_table.shape[1]
    past = n_pages * PAGE_SIZE
    pos = past + jnp.arange(S, dtype=jnp.int32)
    hn = rmsnorm(h, p['attn_norm'])
    q_lat, kv_lat, k_rope, fq, fk, fv, f_logit, gate_a, gate_b = input_projection(hn, p['w_in'])
    c_kv = rmsnorm(kv_lat, p['g_kvlat'])
    q_mla = mla_queries(q_lat, pos, p)
    k_mla, v_mla = mla_keys(c_kv, k_rope, pos, p)
    q_fox, k_fox, v_fox, logf = fox_projections(fq, fk, fv, f_logit, p)
    C_new = jnp.cumsum(logf, axis=1)
    logf_past = cache_fox_logf[li, page_table].reshape(Bd, past, FOX_HEADS).astype(jnp.float32)
    C_past = logf_past - lax.cumsum(logf_past, axis=1, reverse=True)
    offs = jnp.arange(PAGE_SIZE, dtype=jnp.int32)

    def mla_page(j):
        phys = page_table[:, j]
        k, v = mla_keys(cache_mla_latent[li, phys], cache_mla_krope[li, phys], j * PAGE_SIZE + offs, p)
        return k, v, None

    def fox_page(j):
        phys = page_table[:, j]
        k_F = lax.dynamic_slice_in_dim(C_past, j * PAGE_SIZE, PAGE_SIZE, axis=1)
        return cache_fox_k[li, phys], cache_fox_v[li, phys], k_F

    o_mla = paged_attention(q_mla, pos, MLA_SCALE, mla_page, n_pages, k_mla, v_mla)
    o_fox = paged_attention(q_fox, pos, FOX_SCALE, fox_page, n_pages, k_fox, v_fox, C_new, C_new)
    h = h + merge_branches(o_mla, o_fox, gate_a, gate_b, p)
    h = h + moe(rmsnorm(h, p['ffn_norm']), p)
    return h, (c_kv, k_rope, k_fox, v_fox, logf)


def _normal(key, shape, scale=1.0):
    return jax.random.normal(key, shape, jnp.float32) * scale


def _gain(key, shape):
    return 1.0 + 0.05 * jax.random.normal(key, shape, jnp.float32)


def setup_inputs(seed: int = 0) -> dict:
    key = jax.random.key(seed)
    ks = jax.random.split(key, 40)
    n_pages = PAST_LEN // PAGE_SIZE
    n_used = DEC_BATCH * n_pages
    n_pool = n_used * POOL_NUM // POOL_DEN
    page_table = jax.random.permutation(ks[7], n_pool)[:n_used].reshape(DEC_BATCH, n_pages).astype(jnp.int32)
    L = DEPTH
    return {
        'x_prompt': _normal(ks[0], (BATCH, SEQ, D_MODEL)),
        'x_sample': _normal(ks[1], (DEC_BATCH, DEC_SEQ, D_MODEL)),
        'cache_mla_latent': _normal(ks[2], (L, n_pool, PAGE_SIZE, KV_LORA)),
        'cache_mla_krope': _normal(ks[3], (L, n_pool, PAGE_SIZE, ROPE_DIM)),
        'cache_fox_k': _normal(ks[4], (L, n_pool, PAGE_SIZE, FOX_HEADS, FOX_DIM)),
        'cache_fox_v': _normal(ks[5], (L, n_pool, PAGE_SIZE, FOX_HEADS, FOX_DIM)),
        'cache_fox_logf': jax.nn.log_sigmoid(FORGET_BIAS_INIT + _normal(ks[6], (L, n_pool, PAGE_SIZE, FOX_HEADS))),
        'page_table': page_table,
        'meta_tokens': _normal(ks[8], (N_META, D_MODEL)),
        'attn_norm': _gain(ks[9], (L, D_MODEL)),
        'w_in': _normal(ks[10], (L, D_MODEL, IN_COLS), D_MODEL ** -0.5),
        'b_forget': FORGET_BIAS_INIT + _normal(ks[11], (L, FOX_HEADS), 0.1),
        'g_qlat': _gain(ks[12], (L, Q_LORA)),
        'w_uq': _normal(ks[13], (L, Q_LORA, MLA_HEADS, QK_DIM), Q_LORA ** -0.5),
        'g_kvlat': _gain(ks[14], (L, KV_LORA)),
        'w_ukv': _normal(ks[15], (L, KV_LORA, MLA_HEADS, NOPE_DIM + V_DIM), KV_LORA ** -0.5),
        'g_mla_q': _gain(ks[16], (L, QK_DIM)),
        'g_mla_k': _gain(ks[17], (L, QK_DIM)),
        'g_fox_q': _gain(ks[18], (L, FOX_DIM)),
        'g_fox_k': _gain(ks[19], (L, FOX_DIM)),
        'w_branch_mla': _normal(ks[20], (L, MLA_HEADS * V_DIM, D_MODEL), (MLA_HEADS * V_DIM) ** -0.5),
        'w_branch_fox': _normal(ks[21], (L, FOX_WIDTH, D_MODEL), FOX_WIDTH ** -0.5),
        'w_out': _normal(ks[22], (L, D_MODEL, D_MODEL), D_MODEL ** -0.5),
        'ffn_norm': _gain(ks[23], (L, D_MODEL)),
        'w_router': _normal(ks[24], (L, D_MODEL, N_EXPERTS), D_MODEL ** -0.5),
        'b_router': _normal(ks[25], (L, N_EXPERTS), 0.01),
        'w_gate': _normal(ks[26], (L, N_EXPERTS, D_MODEL, D_FF), D_MODEL ** -0.5),
        'b_gate': _normal(ks[27], (L, N_EXPERTS, D_FF), 0.01),
        'w_up': _normal(ks[28], (L, N_EXPERTS, D_MODEL, D_FF), D_MODEL ** -0.5),
        'b_up': _normal(ks[29], (L, N_EXPERTS, D_FF), 0.01),
        'w_down': _normal(ks[30], (L, N_EXPERTS, D_FF, D_MODEL), D_FF ** -0.5),
        'b_down': _normal(ks[31], (L, N_EXPERTS, D_MODEL), 0.01),
    }


def reference(x_prompt, x_sample, cache_mla_latent, cache_mla_krope, cache_fox_k, cache_fox_v, cache_fox_logf,
              page_table, meta_tokens, attn_norm, w_in, b_forget, g_qlat, w_uq, g_kvlat, w_ukv, g_mla_q, g_mla_k,
              g_fox_q, g_fox_k, w_branch_mla, w_branch_fox, w_out, ffn_norm, w_router, b_router, w_gate, b_gate,
              w_up, b_up, w_down, b_down):
    B = x_prompt.shape[0]
    meta = jnp.broadcast_to(meta_tokens.astype(x_prompt.dtype)[None], (B, N_META, D_MODEL))
    h_p = jnp.concatenate([meta, x_prompt], axis=1)
    h_s = x_sample
    new_p = [[] for _ in range(5)]
    new_s = [[] for _ in range(5)]
    for li in range(DEPTH):
        p = {
            'attn_norm': attn_norm[li], 'w_in': w_in[li], 'b_forget': b_forget[li],
            'g_qlat': g_qlat[li], 'w_uq': w_uq[li], 'g_kvlat': g_kvlat[li], 'w_ukv': w_ukv[li],
            'g_mla_q': g_mla_q[li], 'g_mla_k': g_mla_k[li], 'g_fox_q': g_fox_q[li], 'g_fox_k': g_fox_k[li],
            'w_branch_mla': w_branch_mla[li], 'w_branch_fox': w_branch_fox[li], 'w_out': w_out[li],
            'ffn_norm': ffn_norm[li], 'w_router': w_router[li], 'b_router': b_router[li],
            'w_gate': w_gate[li], 'b_gate': b_gate[li], 'w_up': w_up[li], 'b_up': b_up[li],
            'w_down': w_down[li], 'b_down': b_down[li],
        }
        h_p, st_p = prompt_layer(h_p, p, li == DEPTH - 1)
        h_s, st_s = sample_layer(h_s, p, li, cache_mla_latent, cache_mla_krope, cache_fox_k, cache_fox_v,
                                 cache_fox_logf, page_table)
        for i in range(5):
            new_p[i].append(st_p[i])
            new_s[i].append(st_s[i])
    y_prompt = h_p
    y_sample = h_s
    lat_p, kr_p, fk_p, fv_p, lf_p = [jnp.stack(a, axis=0) for a in new_p]
    lat_s, kr_s, fk_s, fv_s, lf_s = [jnp.stack(a, axis=0) for a in new_s]
    return (y_prompt, y_sample, lat_p, kr_p, fk_p, fv_p, lf_p, lat_s, kr_s, fk_s, fv_s, lf_s)
```

```python
import functools

import jax
import jax.numpy as jnp
import numpy as np
from jax import lax
from jax.experimental import pallas as pl
from jax.experimental.pallas import tpu as pltpu

N_META = 16
MLA_HEADS = 8
NOPE_DIM = 64
ROPE_DIM = 32
V_DIM = 64
QK_DIM = NOPE_DIM + ROPE_DIM
ROPE_THETA = 10000.0
MLA_SCALE = QK_DIM ** -0.5
FOX_HEADS = 8
FOX_DIM = 64
FOX_SCALE = FOX_DIM ** -0.5
TOP_K = 4
SWIGLU_ALPHA = 1.702
SWIGLU_LIMIT = 7.0
NORM_EPS = 1e-6
PAGE_SIZE = 128

LANES = 128
VMEM_LIMIT = 56 * 1024 * 1024
HEAD_SLOT = LANES
NEG_INF = float("-inf")

F32 = jnp.float32
BF16 = jnp.bfloat16


def _dot(a, b):
    return jnp.dot(a, b, preferred_element_type=F32)


def _dot_nt(a, b):
    return lax.dot_general(a, b, (((1,), (1,)), ((), ())), preferred_element_type=F32)


def _const_spec(shape):
    n = len(shape)
    return pl.BlockSpec(shape, lambda *_: (0,) * n)


def _rope_lanes(x, c, s1, s2):
    return x * c + pltpu.roll(x, LANES - ROPE_DIM // 2, 1) * s1 + pltpu.roll(x, ROPE_DIM // 2, 1) * s2


def _proj_body(x_ref, c_ref, s1_ref, s2_ref, tri_ref, an_ref, wq_ref, gql_ref, wuq_ref, gq_ref,
               wkv_ref, gkv_ref, wkr0_ref, wkr64_ref, wuk_ref, gk_ref, wuv_ref,
               wfq_ref, gfq_ref, wfk_ref, gfk_ref, wfv_ref, wf3_ref, bf3_ref, wga_ref, wgb_ref,
               ckv_o, kr_o, kfox_o, vfox_o, logf_o, fcol_o, nf_o,
               qm_o, km_o, vme_o, vmo_o, qf_o, kf_o, vfe_o, vfo_o, sga_o, sgb_o,
               carry_scr):
    tm = x_ref.shape[0]
    x = x_ref[...]
    xn = (x * lax.rsqrt(jnp.mean(x * x, axis=-1, keepdims=True) + NORM_EPS) * an_ref[...]).astype(BF16)
    c, s1, s2 = c_ref[...], s1_ref[...], s2_ref[...]
    lane = lax.broadcasted_iota(jnp.int32, (tm, LANES), 1)
    low = lane < FOX_DIM

    ql = _dot(xn, wq_ref[...])
    qn = (ql * lax.rsqrt(jnp.mean(ql * ql, axis=-1, keepdims=True) + NORM_EPS) * gql_ref[...]).astype(BF16)
    q = _dot(qn, wuq_ref[...])
    gq = gq_ref[...]
    for h in range(MLA_HEADS):
        qh = q[:, h * HEAD_SLOT:(h + 1) * HEAD_SLOT]
        r = lax.rsqrt(jnp.sum(qh * qh, axis=-1, keepdims=True) * (1.0 / QK_DIM) + NORM_EPS)
        qm_o[:, h * HEAD_SLOT:(h + 1) * HEAD_SLOT] = _rope_lanes(qh * r * gq, c, s1, s2).astype(BF16)

    kvl = _dot(xn, wkv_ref[...])
    ckv = kvl * lax.rsqrt(jnp.mean(kvl * kvl, axis=-1, keepdims=True) + NORM_EPS) * gkv_ref[...]
    ckv_o[...] = ckv
    c16 = ckv.astype(BF16)
    kr_o[...] = _dot(xn, wkr0_ref[...])[:, :ROPE_DIM]
    kr = _dot(xn, wkr64_ref[...])
    ssq_r = jnp.sum(kr * kr, axis=-1, keepdims=True)
    gk = gk_ref[...]
    kr_rot = _rope_lanes(kr * gk, c, s1, s2)
    kn = _dot(c16, wuk_ref[...])
    for h in range(MLA_HEADS):
        kh = kn[:, h * HEAD_SLOT:(h + 1) * HEAD_SLOT]
        r = lax.rsqrt((jnp.sum(kh * kh, axis=-1, keepdims=True) + ssq_r) * (1.0 / QK_DIM) + NORM_EPS)
        km_o[:, h * HEAD_SLOT:(h + 1) * HEAD_SLOT] = ((kh * gk + kr_rot) * r).astype(BF16)
    vm = _dot(c16, wuv_ref[...])
    for p in range(MLA_HEADS // 2):
        vp = vm[:, p * LANES:(p + 1) * LANES]
        vme_o[:, p * LANES:(p + 1) * LANES] = jnp.where(low, vp, 0.0).astype(BF16)
        vmo_o[:, p * LANES:(p + 1) * LANES] = jnp.where(low, 0.0, vp).astype(BF16)

    def pair_norm(t):
        sq = t * t
        lo = jnp.sum(jnp.where(low, sq, 0.0), axis=-1, keepdims=True)
        hi = jnp.sum(jnp.where(low, 0.0, sq), axis=-1, keepdims=True)
        r_lo = lax.rsqrt(lo * (1.0 / FOX_DIM) + NORM_EPS)
        r_hi = lax.rsqrt(hi * (1.0 / FOX_DIM) + NORM_EPS)
        return t * jnp.where(low, r_lo, r_hi)

    fq = _dot(xn, wfq_ref[...])
    fk = _dot(xn, wfk_ref[...])
    fv = _dot(xn, wfv_ref[...])
    vfox_o[...] = fv
    for p in range(FOX_HEADS // 2):
        sl = slice(p * LANES, (p + 1) * LANES)
        qp = pair_norm(fq[:, sl]) * gfq_ref[:, sl]
        qf_o[:, (2 * p) * HEAD_SLOT:(2 * p + 1) * HEAD_SLOT] = jnp.where(low, qp, 0.0).astype(BF16)
        qf_o[:, (2 * p + 1) * HEAD_SLOT:(2 * p + 2) * HEAD_SLOT] = jnp.where(low, 0.0, qp).astype(BF16)
        kp = pair_norm(fk[:, sl]) * gfk_ref[:, sl]
        kfox_o[:, sl] = kp
        kf_o[:, sl] = kp.astype(BF16)
        vp = fv[:, sl]
        vfe_o[:, sl] = jnp.where(low, vp, 0.0).astype(BF16)
        vfo_o[:, sl] = jnp.where(low, 0.0, vp).astype(BF16)

    fl = _dot(xn, wf3_ref[...]) + bf3_ref[...]
    lf = jnp.minimum(fl, 0.0) - jnp.log(1.0 + jnp.exp(-jnp.abs(fl)))
    logf_o[...] = lf[:, :FOX_HEADS]
    hi = lf.astype(BF16)
    r1 = lf - hi.astype(F32)
    mid = r1.astype(BF16)
    lo = (r1 - mid.astype(F32)).astype(BF16)
    piece = jnp.where(lane < FOX_HEADS, hi.astype(F32), jnp.where(lane < 2 * FOX_HEADS, mid.astype(F32), lo.astype(F32)))
    piece = jnp.where(lane < 3 * FOX_HEADS, piece, 0.0).astype(BF16)
    f3 = _dot(tri_ref[...], piece)
    f = f3 + pltpu.roll(f3, LANES - FOX_HEADS, 1) + pltpu.roll(f3, LANES - 2 * FOX_HEADS, 1)

    @pl.when(pl.program_id(1) == 0)
    def _():
        carry_scr[...] = jnp.zeros_like(carry_scr)

    f = f + carry_scr[0:1, :]
    carry_scr[0:1, :] = f[tm - 1:tm, :]
    fcol_o[...] = f[:, :FOX_HEADS]
    nf_o[0] = jnp.transpose(-f)[:FOX_HEADS, :]

    sga_o[...] = jax.nn.sigmoid(_dot(xn, wga_ref[...])).astype(BF16)
    sgb_o[...] = jax.nn.sigmoid(_dot(xn, wgb_ref[...])).astype(BF16)


def _proj(x, tables, tri, pw, nb, nt, tm):
    rows, d = x.shape
    c_kv = pw["wkv"].shape[1]
    fw = pw["wfq"].shape[1]
    slots = MLA_HEADS * HEAD_SLOT

    def row_spec(width):
        return pl.BlockSpec((tm, width), lambda b, i: (b * nt + i, 0))

    tab_spec = pl.BlockSpec((tm, LANES), lambda b, i: (i, 0))
    weights = [pw[k] for k in ("an", "wq", "gql", "wuq", "gq", "wkv", "gkv", "wkr0", "wkr64", "wuk", "gk",
                               "wuv", "wfq", "gfq", "wfk", "gfk", "wfv", "wf3", "bf3", "wga", "wgb")]
    in_specs = [row_spec(d), tab_spec, tab_spec, tab_spec, _const_spec(tri.shape)]
    in_specs += [_const_spec(w.shape) for w in weights]
    out_shapes = [
        jax.ShapeDtypeStruct((rows, c_kv), F32),
        jax.ShapeDtypeStruct((rows, ROPE_DIM), F32),
        jax.ShapeDtypeStruct((rows, fw), F32),
        jax.ShapeDtypeStruct((rows, fw), F32),
        jax.ShapeDtypeStruct((rows, FOX_HEADS), F32),
        jax.ShapeDtypeStruct((rows, FOX_HEADS), F32),
        jax.ShapeDtypeStruct((nb, FOX_HEADS, nt * tm), F32),
        jax.ShapeDtypeStruct((rows, slots), BF16),
        jax.ShapeDtypeStruct((rows, slots), BF16),
        jax.ShapeDtypeStruct((rows, fw), BF16),
        jax.ShapeDtypeStruct((rows, fw), BF16),
        jax.ShapeDtypeStruct((rows, slots), BF16),
        jax.ShapeDtypeStruct((rows, fw), BF16),
        jax.ShapeDtypeStruct((rows, fw), BF16),
        jax.ShapeDtypeStruct((rows, fw), BF16),
        jax.ShapeDtypeStruct((rows, d), BF16),
        jax.ShapeDtypeStruct((rows, d), BF16),
    ]
    out_specs = [row_spec(s.shape[1]) for s in out_shapes]
    out_specs[6] = pl.BlockSpec((1, FOX_HEADS, tm), lambda b, i: (b, 0, i))
    return pl.pallas_call(
        _proj_body,
        grid=(nb, nt),
        in_specs=in_specs,
        out_specs=out_specs,
        out_shape=out_shapes,
        scratch_shapes=[pltpu.VMEM((8, LANES), F32)],
        compiler_params=pltpu.CompilerParams(dimension_semantics=("arbitrary", "arbitrary"),
                                             vmem_limit_bytes=VMEM_LIMIT),
        name="proj",
    )(x, *tables, tri, *weights)


def _flash_body(q_ref, k_ref, ve_ref, vo_ref, km_ref, vem_ref, vom_ref, *rest, k_per_slot, has_bias, tb):
    if has_bias:
        nf_ref, nfm_ref, o_ref = rest
    else:
        (o_ref,) = rest
    qi = pl.program_id(1)
    n_meta = km_ref.shape[0]
    row = lax.broadcasted_iota(jnp.int32, (tb, tb), 0)
    col = lax.broadcasted_iota(jnp.int32, (tb, tb), 1)
    tri_mask = col <= row
    meta_mask = lax.broadcasted_iota(jnp.int32, (tb, n_meta), 1) < N_META

    for h in range(MLA_HEADS):
        ks = (h // k_per_slot) * LANES
        vs = (h // 2) * LANES
        v_ref, vm_ref = (ve_ref, vem_ref) if h % 2 == 0 else (vo_ref, vom_ref)
        qh = q_ref[0, :, h * HEAD_SLOT:(h + 1) * HEAD_SLOT]

        def update(carry, s, v):
            m, l, acc = carry
            m_new = jnp.maximum(m, jnp.max(s, axis=-1, keepdims=True))
            alpha = jnp.exp(m - m_new)
            p = jnp.exp(s - m_new)
            l = l * alpha + jnp.sum(p, axis=-1, keepdims=True)
            acc = acc * alpha + _dot(p.astype(BF16), v)
            return m_new, l, acc

        s = _dot_nt(qh, km_ref[:, ks:ks + LANES])
        if has_bias:
            s = s + nfm_ref[h:h + 1, :]
        s = jnp.where(meta_mask, s, NEG_INF)
        carry = (jnp.full((tb, 1), NEG_INF, F32), jnp.zeros((tb, 1), F32), jnp.zeros((tb, LANES), F32))
        carry = update(carry, s, vm_ref[:, vs:vs + LANES])

        def block(j, carry, masked):
            start = pl.multiple_of(j * tb, tb)
            s = _dot_nt(qh, k_ref[0, pl.ds(start, tb), ks:ks + LANES])
            if has_bias:
                s = s + nf_ref[0, j, h:h + 1, :]
            if masked:
                s = jnp.where(tri_mask, s, NEG_INF)
            return update(carry, s, v_ref[0, pl.ds(start, tb), vs:vs + LANES])

        carry = lax.fori_loop(0, qi, lambda j, cr: block(j, cr, False), carry)
        m, l, acc = block(qi, carry, True)
        out = acc * (1.0 / l)
        if h % 2 == 0:
            pair = out
        else:
            o_ref[0, :, vs:vs + LANES] = (pair + out).astype(o_ref.dtype)


def _flash(q, k, ve, vo, km, vem, vom, nf, nfm, k_per_slot, tb):
    b, t, _ = q.shape
    has_bias = nf is not None
    nq = t // tb
    in_specs = [
        pl.BlockSpec((1, tb, q.shape[2]), lambda bi, i: (bi, i, 0)),
        pl.BlockSpec((1, t, k.shape[2]), lambda bi, i: (bi, 0, 0)),
        pl.BlockSpec((1, t, ve.shape[2]), lambda bi, i: (bi, 0, 0)),
        pl.BlockSpec((1, t, vo.shape[2]), lambda bi, i: (bi, 0, 0)),
        _const_spec(km.shape), _const_spec(vem.shape), _const_spec(vom.shape),
    ]
    args = [q, k, ve, vo, km, vem, vom]
    if has_bias:
        in_specs += [pl.BlockSpec((1, nq, FOX_HEADS, tb), lambda bi, i: (bi, 0, 0, 0)), _const_spec(nfm.shape)]
        args += [nf, nfm]
    return pl.pallas_call(
        functools.partial(_flash_body, k_per_slot=k_per_slot, has_bias=has_bias, tb=tb),
        grid=(b, nq),
        in_specs=in_specs,
        out_specs=pl.BlockSpec((1, tb, ve.shape[2]), lambda bi, i: (bi, i, 0)),
        out_shape=jax.ShapeDtypeStruct((b, t, ve.shape[2]), BF16),
        compiler_params=pltpu.CompilerParams(dimension_semantics=("arbitrary", "arbitrary"),
                                             vmem_limit_bytes=VMEM_LIMIT),
        name="flash_fox" if has_bias else "flash_mla",
    )(*args)


def _softmax_step(s, m_scr, l_scr):
    m_old = m_scr[...]
    m_new = jnp.maximum(m_old, jnp.max(s, axis=-1, keepdims=True))
    alpha = jnp.exp(m_old - m_new)
    p = jnp.exp(s - m_new)
    l_scr[...] = l_scr[...] * alpha + jnp.sum(p, axis=-1, keepdims=True)
    m_scr[...] = m_new
    return p, alpha


def _new_token_mask(rows, n_new):
    t = lax.broadcasted_iota(jnp.int32, (rows, PAGE_SIZE), 1)
    qidx = lax.broadcasted_iota(jnp.int32, (rows, PAGE_SIZE), 0) // MLA_HEADS
    return (t <= qidx) & (t < n_new)


def _fold_heads(full, n_q, width):
    rows = full.shape[0]
    lane_head = lax.broadcasted_iota(jnp.int32, (rows, full.shape[1]), 1) // width
    row_head = lax.broadcasted_iota(jnp.int32, (rows, full.shape[1]), 0) % MLA_HEADS
    kept = jnp.where(lane_head == row_head, full, 0.0)
    return jnp.sum(kept.reshape(n_q, MLA_HEADS, full.shape[1]), axis=1)


def _mla_dec_body(pt_ref, *refs, pp, n_q):
    lat_refs = refs[:pp]
    kr_refs = refs[pp:2 * pp]
    (cos_ref, sin_ref, qn_ref, qr_ref, latn_ref, krn_ref, cosn_ref, sinn_ref,
     gkn_ref, gkr_ref, wuk_ref, wukt_ref, wuv_ref, o_ref, m_scr, l_scr, acc_scr, qa_scr) = refs[2 * pp:]
    g = pl.program_id(1)
    rows = n_q * MLA_HEADS
    half = ROPE_DIM // 2

    @pl.when(g == 0)
    def _():
        m_scr[...] = jnp.full_like(m_scr, NEG_INF)
        l_scr[...] = jnp.zeros_like(l_scr)
        acc_scr[...] = jnp.zeros_like(acc_scr)
        qn = (qn_ref[0].astype(F32) * gkn_ref[...]).astype(BF16)
        qa_scr[...] = _dot_nt(qn, wuk_ref[...]).astype(BF16)

    def page(lat, krt, cos_t, sin_t, mask):
        c16 = lat.astype(BF16)
        knt = _dot_nt(wukt_ref[...], c16)
        ssq_n = jnp.sum((knt * knt).reshape(MLA_HEADS, NOPE_DIM, PAGE_SIZE), axis=1)
        ssq_r = jnp.sum(krt * krt, axis=0, keepdims=True)
        r = lax.rsqrt((ssq_n + ssq_r) * (1.0 / QK_DIM) + NORM_EPS)
        kg = krt * gkr_ref[...]
        x1, x2 = kg[:half], kg[half:]
        k_rot = jnp.concatenate([x1 * cos_t - x2 * sin_t, x2 * cos_t + x1 * sin_t], axis=0).astype(BF16)
        s = _dot_nt(qa_scr[...], c16) + _dot(qr_ref[0], k_rot)
        s = s * jnp.concatenate([r] * n_q, axis=0)
        if mask is not None:
            s = jnp.where(mask, s, NEG_INF)
        p, alpha = _softmax_step(s, m_scr, l_scr)
        acc_scr[...] = acc_scr[...] * alpha + _dot(p.astype(BF16), c16)

    for i in range(pp):
        page(lat_refs[i][0], kr_refs[i][0], cos_ref[i], sin_ref[i], None)

    @pl.when(g == pl.num_programs(1) - 1)
    def _():
        page(latn_ref[0], krn_ref[0], cosn_ref[...], sinn_ref[...], _new_token_mask(rows, n_q))
        o_lat = (acc_scr[...] * (1.0 / l_scr[...])).astype(BF16)
        o_ref[0] = _fold_heads(_dot(o_lat, wuv_ref[...]), n_q, V_DIM).astype(o_ref.dtype)


def _mla_decode(page_table, cache_lat, cache_kr, cos_t, sin_t, qn_bd, qr, lat_new, kr_new, cos_new, sin_new,
                gkn, gkr, wuk, wukt, wuv, pp):
    bd, n_pages = page_table.shape
    n_q = qr.shape[1] // MLA_HEADS
    rows = qr.shape[1]
    c_kv = cache_lat.shape[-1]

    def page_spec(shape, i):
        return pl.BlockSpec((1,) + shape, lambda b, g, pt: (pt[b, g * pp + i], 0, 0))

    def batch_spec(shape):
        return pl.BlockSpec((1,) + shape, lambda b, g, pt: (b, 0, 0))

    def const_spec(shape):
        n = len(shape)
        return pl.BlockSpec(shape, lambda b, g, pt: (0,) * n)

    in_specs = [page_spec((PAGE_SIZE, c_kv), i) for i in range(pp)]
    in_specs += [page_spec((ROPE_DIM, PAGE_SIZE), i) for i in range(pp)]
    in_specs += [
        pl.BlockSpec((pp, ROPE_DIM // 2, PAGE_SIZE), lambda b, g, pt: (g, 0, 0)),
        pl.BlockSpec((pp, ROPE_DIM // 2, PAGE_SIZE), lambda b, g, pt: (g, 0, 0)),
        batch_spec(qn_bd.shape[1:]), batch_spec(qr.shape[1:]),
        batch_spec(lat_new.shape[1:]), batch_spec(kr_new.shape[1:]),
        const_spec(cos_new.shape), const_spec(sin_new.shape),
        const_spec(gkn.shape), const_spec(gkr.shape), const_spec(wuk.shape), const_spec(wukt.shape),
        const_spec(wuv.shape),
    ]
    grid_spec = pltpu.PrefetchScalarGridSpec(
        num_scalar_prefetch=1,
        grid=(bd, n_pages // pp),
        in_specs=in_specs,
        out_specs=pl.BlockSpec((1, n_q, wuv.shape[1]), lambda b, g, pt: (b, 0, 0)),
        scratch_shapes=[pltpu.VMEM((rows, 1), F32), pltpu.VMEM((rows, 1), F32),
                        pltpu.VMEM((rows, c_kv), F32), pltpu.VMEM((rows, c_kv), BF16)],
    )
    return pl.pallas_call(
        functools.partial(_mla_dec_body, pp=pp, n_q=n_q),
        grid_spec=grid_spec,
        out_shape=jax.ShapeDtypeStruct((bd, n_q, wuv.shape[1]), BF16),
        compiler_params=pltpu.CompilerParams(dimension_semantics=("arbitrary", "arbitrary"),
                                             vmem_limit_bytes=VMEM_LIMIT),
        name="mla_decode",
    )(page_table, *([cache_lat] * pp), *([cache_kr] * pp), cos_t, sin_t, qn_bd, qr, lat_new, kr_new,
      cos_new, sin_new, gkn, gkr, wuk, wukt, wuv)


def _fox_dec_body(pt_ref, *refs, pp, n_q):
    k_refs = refs[:pp]
    v_refs = refs[pp:2 * pp]
    lf_refs = refs[2 * pp:3 * pp]
    us_ref, q_ref, kn_ref, vn_ref, sufn_ref, o_ref, m_scr, l_scr, acc_scr, suf_scr = refs[3 * pp:]
    g = pl.program_id(1)
    rows = n_q * FOX_HEADS

    def page(kt, vt, bias, mask):
        s = _dot(q_ref[0], kt.astype(BF16)) + jnp.concatenate([bias] * n_q, axis=0)
        if mask is not None:
            s = jnp.where(mask, s, NEG_INF)
        p, alpha = _softmax_step(s, m_scr, l_scr)
        acc_scr[...] = acc_scr[...] * alpha + _dot_nt(p.astype(BF16), vt.astype(BF16))

    @pl.when(g == 0)
    def _():
        m_scr[...] = jnp.full_like(m_scr, NEG_INF)
        l_scr[...] = jnp.zeros_like(l_scr)
        acc_scr[...] = jnp.zeros_like(acc_scr)
        suf_scr[...] = jnp.zeros_like(suf_scr)
        page(kn_ref[0], vn_ref[0], sufn_ref[0], _new_token_mask(rows, n_q))

    for i in reversed(range(pp)):
        lf = lf_refs[i][0]
        hi = lf.astype(BF16).astype(F32)
        mid = (lf - hi).astype(BF16).astype(F32)
        lo = lf - hi - mid
        w = _dot(jnp.concatenate([hi, mid, lo], axis=0).astype(BF16), us_ref[...])
        within = w[0:FOX_HEADS] + w[FOX_HEADS:2 * FOX_HEADS] + w[2 * FOX_HEADS:3 * FOX_HEADS]
        page(k_refs[i][0], v_refs[i][0], within + suf_scr[...], None)
        suf_scr[...] = suf_scr[...] + jnp.sum(lf, axis=1, keepdims=True)

    @pl.when(g == pl.num_programs(1) - 1)
    def _():
        out = acc_scr[...] * (1.0 / l_scr[...])
        o_ref[0] = _fold_heads(out, n_q, FOX_DIM).astype(o_ref.dtype)


def _fox_decode(page_table, cache_kt, cache_vt, cache_lf, q_bd, kt_new, vt_new, suf_new, pp):
    bd, n_pages = page_table.shape
    rows = q_bd.shape[1]
    n_q = rows // FOX_HEADS
    fw = cache_kt.shape[1]
    ng = n_pages // pp
    i_tok = np.arange(PAGE_SIZE)
    u_strict = jnp.asarray(i_tok[:, None] > i_tok[None, :], dtype=BF16)

    def page_spec(shape, i):
        return pl.BlockSpec((1,) + shape, lambda b, g, pt: (pt[b, (ng - 1 - g) * pp + i], 0, 0))

    def batch_spec(shape):
        return pl.BlockSpec((1,) + shape, lambda b, g, pt: (b,) + (0,) * len(shape))

    in_specs = [page_spec((fw, PAGE_SIZE), i) for i in range(pp)]
    in_specs += [page_spec((fw, PAGE_SIZE), i) for i in range(pp)]
    in_specs += [page_spec((FOX_HEADS, PAGE_SIZE), i) for i in range(pp)]
    in_specs += [
        pl.BlockSpec(u_strict.shape, lambda b, g, pt: (0, 0)),
        batch_spec(q_bd.shape[1:]), batch_spec(kt_new.shape[1:]), batch_spec(vt_new.shape[1:]),
        batch_spec(suf_new.shape[1:]),
    ]
    grid_spec = pltpu.PrefetchScalarGridSpec(
        num_scalar_prefetch=1,
        grid=(bd, ng),
        in_specs=in_specs,
        out_specs=pl.BlockSpec((1, n_q, fw), lambda b, g, pt: (b, 0, 0)),
        scratch_shapes=[pltpu.VMEM((rows, 1), F32), pltpu.VMEM((rows, 1), F32), pltpu.VMEM((rows, fw), F32),
                        pltpu.VMEM((FOX_HEADS, PAGE_SIZE), F32)],
    )
    return pl.pallas_call(
        functools.partial(_fox_dec_body, pp=pp, n_q=n_q),
        grid_spec=grid_spec,
        out_shape=jax.ShapeDtypeStruct((bd, n_q, fw), BF16),
        compiler_params=pltpu.CompilerParams(dimension_semantics=("arbitrary", "arbitrary"),
                                             vmem_limit_bytes=VMEM_LIMIT),
        name="fox_decode",
    )(page_table, *([cache_kt] * pp), *([cache_vt] * pp), *([cache_lf] * pp), u_strict, q_bd, kt_new, vt_new,
      suf_new)


def _merge_body(h_ref, oa_ref, ob_ref, sga_ref, sgb_ref, wba_ref, wbb_ref, wo_ref, fn_ref, wr_ref, br_ref,
                h2_o, xn_o, idx_o, gate_o, *, n_experts):
    tm = h_ref.shape[0]
    ua = _dot(oa_ref[...], wba_ref[...]) * sga_ref[...].astype(F32)
    ub = _dot(ob_ref[...], wbb_ref[...]) * sgb_ref[...].astype(F32)
    h2 = h_ref[...] + _dot((ua + ub).astype(BF16), wo_ref[...])
    h2_o[...] = h2
    xn = h2 * lax.rsqrt(jnp.mean(h2 * h2, axis=-1, keepdims=True) + NORM_EPS) * fn_ref[...]
    xn_o[...] = xn.astype(BF16)
    logits = jnp.dot(xn, wr_ref[...], precision=lax.Precision.HIGHEST, preferred_element_type=F32) + br_ref[...]
    lane = lax.broadcasted_iota(jnp.int32, (tm, LANES), 1).astype(F32)
    logits = jnp.where(lane < n_experts, logits, NEG_INF)
    idx_acc = jnp.zeros((tm, LANES), F32)
    val_acc = jnp.full((tm, LANES), NEG_INF, F32)
    for k in range(TOP_K):
        top = jnp.max(logits, axis=-1, keepdims=True)
        arg = jnp.min(jnp.where(logits == top, lane, float(LANES)), axis=-1, keepdims=True)
        idx_acc = jnp.where(lane == k, arg, idx_acc)
        val_acc = jnp.where(lane == k, top, val_acc)
        logits = jnp.where(lane == arg, NEG_INF, logits)
    e = jnp.exp(val_acc - jnp.max(val_acc, axis=-1, keepdims=True))
    idx_o[...] = idx_acc.astype(jnp.int32)
    gate_o[...] = e / jnp.sum(e, axis=-1, keepdims=True)


def _merge(h, oa, ob, sga, sgb, mw, tm):
    rows, d = h.shape
    n_experts = mw["n_experts"]

    def row_spec(width):
        return pl.BlockSpec((tm, width), lambda i: (i, 0))

    weights = [mw[k] for k in ("wba", "wbb", "wo", "fn", "wr", "br")]
    return pl.pallas_call(
        functools.partial(_merge_body, n_experts=n_experts),
        grid=(rows // tm,),
        in_specs=[row_spec(d), row_spec(oa.shape[1]), row_spec(ob.shape[1]), row_spec(d), row_spec(d)]
        + [_const_spec(w.shape) for w in weights],
        out_specs=[row_spec(d), row_spec(d), row_spec(LANES), row_spec(LANES)],
        out_shape=[jax.ShapeDtypeStruct((rows, d), F32), jax.ShapeDtypeStruct((rows, d), BF16),
                   jax.ShapeDtypeStruct((rows, LANES), jnp.int32), jax.ShapeDtypeStruct((rows, LANES), F32)],
        compiler_params=pltpu.CompilerParams(dimension_semantics=("arbitrary",), vmem_limit_bytes=VMEM_LIMIT),
        name="merge",
    )(h, oa, ob, sga, sgb, *weights)


def _ffn_body(be_ref, nu_ref, x_ref, w_ref, wg_ref, bg_ref, wu_ref, bu_ref, wd_ref, bd_ref, o_ref):
    @pl.when(pl.program_id(0) < nu_ref[0])
    def _():
        x = x_ref[...]
        g = _dot(x, wg_ref[0].astype(BF16)) + bg_ref[0]
        u = _dot(x, wu_ref[0].astype(BF16)) + bu_ref[0]
        g = jnp.minimum(g, SWIGLU_LIMIT)
        u = jnp.clip(u, -SWIGLU_LIMIT, SWIGLU_LIMIT)
        hid = ((u + 1.0) * g * jax.nn.sigmoid(SWIGLU_ALPHA * g)).astype(BF16)
        y = _dot(hid, wd_ref[0].astype(BF16)) + bd_ref[0]
        o_ref[...] = (y * w_ref[...]).astype(o_ref.dtype)


def _ffn(block_e, n_used, x_rows, w_rows, w_gate, b_gate, w_up, b_up, w_down, b_down, tb):
    n_rows, d = x_rows.shape
    e, _, f = w_gate.shape

    def wspec(shape):
        return pl.BlockSpec((1,) + shape, lambda i, be, nu: (be[i], 0, 0))

    grid_spec = pltpu.PrefetchScalarGridSpec(
        num_scalar_prefetch=2,
        grid=(n_rows // tb,),
        in_specs=[pl.BlockSpec((tb, d), lambda i, be, nu: (i, 0)),
                  pl.BlockSpec((tb, 1), lambda i, be, nu: (i, 0)),
                  wspec((d, f)), wspec((1, f)), wspec((d, f)), wspec((1, f)), wspec((f, d)), wspec((1, d))],
        out_specs=pl.BlockSpec((tb, d), lambda i, be, nu: (i, 0)),
    )
    return pl.pallas_call(
        _ffn_body,
        grid_spec=grid_spec,
        out_shape=jax.ShapeDtypeStruct((n_rows, d), BF16),
        compiler_params=pltpu.CompilerParams(dimension_semantics=("arbitrary",), vmem_limit_bytes=VMEM_LIMIT),
        name="ffn",
    )(block_e, n_used, x_rows, w_rows, w_gate, b_gate.reshape(e, 1, f), w_up, b_up.reshape(e, 1, f),
      w_down, b_down.reshape(e, 1, d))


def _rope_tables(pos):
    half = ROPE_DIM // 2
    inv_freq = ROPE_THETA ** (-jnp.arange(half, dtype=F32) / half)
    ang = pos.astype(F32)[:, None] * inv_freq[None, :]
    cos, sin = jnp.cos(ang), jnp.sin(ang)
    n = pos.shape[0]
    zeros = jnp.zeros((n, half), F32)
    c = jnp.concatenate([jnp.ones((n, NOPE_DIM), F32), cos, cos, jnp.ones((n, LANES - QK_DIM), F32)], axis=1)
    s1 = jnp.concatenate([jnp.zeros((n, NOPE_DIM), F32), -sin, zeros, jnp.zeros((n, LANES - QK_DIM), F32)], axis=1)
    s2 = jnp.concatenate([jnp.zeros((n, NOPE_DIM), F32), zeros, sin, jnp.zeros((n, LANES - QK_DIM), F32)], axis=1)
    return c, s1, s2


def _pad_cols(w, width, offset=0):
    return jnp.pad(w, ((0, 0), (offset, width - offset - w.shape[1])))


def _proj_weights(attn_norm, w_in, b_forget, g_qlat, w_uq, g_kvlat, w_ukv, g_mla_q, g_mla_k, g_fox_q, g_fox_k):
    q_lora, c_kv = w_uq.shape[0], w_ukv.shape[0]
    fw = FOX_HEADS * FOX_DIM
    d = w_in.shape[0]
    sizes = (q_lora, c_kv, ROPE_DIM, fw, fw, fw, FOX_HEADS, d, d)
    offs = np.concatenate([[0], np.cumsum(sizes)])
    wq, wkv, wkr, wfq, wfk, wfv, wf, wga, wgb = [w_in[:, offs[i]:offs[i + 1]] for i in range(len(sizes))]
    pad_head = LANES - QK_DIM
    wuq = jnp.pad(w_uq, ((0, 0), (0, 0), (0, pad_head))).reshape(q_lora, MLA_HEADS * HEAD_SLOT)
    wuk = jnp.pad(w_ukv[:, :, :NOPE_DIM], ((0, 0), (0, 0), (0, LANES - NOPE_DIM))).reshape(c_kv, MLA_HEADS * HEAD_SLOT)
    wuv = w_ukv[:, :, NOPE_DIM:].reshape(c_kv, MLA_HEADS * V_DIM)
    row = lambda v: v.reshape(1, -1).astype(F32)
    return {
        "an": row(attn_norm), "wq": wq.astype(BF16), "gql": row(g_qlat), "wuq": wuq.astype(BF16),
        "gq": _pad_cols(row(g_mla_q) * MLA_SCALE, LANES),
        "wkv": wkv.astype(BF16), "gkv": row(g_kvlat),
        "wkr0": _pad_cols(wkr, LANES).astype(BF16), "wkr64": _pad_cols(wkr, LANES, NOPE_DIM).astype(BF16),
        "wuk": wuk.astype(BF16), "gk": _pad_cols(row(g_mla_k), LANES), "wuv": wuv.astype(BF16),
        "wfq": wfq.astype(BF16), "gfq": jnp.tile(row(g_fox_q), (1, FOX_HEADS)) * FOX_SCALE,
        "wfk": wfk.astype(BF16), "gfk": jnp.tile(row(g_fox_k), (1, FOX_HEADS)),
        "wfv": wfv.astype(BF16),
        "wf3": _pad_cols(jnp.tile(wf, (1, 3)), LANES).astype(BF16),
        "bf3": _pad_cols(jnp.tile(row(b_forget), (1, 3)), LANES),
        "wga": wga.astype(BF16), "wgb": wgb.astype(BF16),
    }


def _tri(n, group=None):
    i = np.arange(n)
    m = i[None, :] <= i[:, None]
    if group is not None:
        m &= (i[None, :] // group) == (i[:, None] // group)
    return jnp.asarray(m, dtype=BF16)


def _pick_tile(n, target):
    t = min(n, target)
    while n % t:
        t //= 2
    return t


def _moe(xn, top_idx, gates, w_gate, b_gate, w_up, b_up, w_down, b_down, tb):
    n_tok, d = xn.shape
    n_experts = w_gate.shape[0]
    n_assign = n_tok * TOP_K
    flat_e = top_idx.reshape(-1)
    order = jnp.argsort(flat_e)
    counts = jnp.bincount(flat_e, length=n_experts).astype(jnp.int32)
    padded = (counts + tb - 1) // tb * tb
    pad_end = jnp.cumsum(padded)
    pad_start = pad_end - padded
    start = jnp.cumsum(counts) - counts
    n_blocks = -(-n_assign // tb) + n_experts
    n_used = (pad_end[-1] // tb).astype(jnp.int32).reshape(1)
    block_start = jnp.arange(n_blocks, dtype=jnp.int32) * tb
    block_e = jnp.minimum(jnp.searchsorted(pad_end, block_start, side="right"), n_experts - 1).astype(jnp.int32)
    r = jnp.arange(n_blocks * tb, dtype=jnp.int32)
    r_e = jnp.repeat(block_e, tb)
    j = r - pad_start[r_e]
    valid = j < counts[r_e]
    src_assign = order[jnp.clip(start[r_e] + j, 0, n_assign - 1)]
    x_rows = xn[src_assign // TOP_K]
    w_rows = jnp.where(valid, gates.reshape(-1)[src_assign], 0.0).reshape(-1, 1)
    y_rows = _ffn(block_e, n_used, x_rows, w_rows, w_gate, b_gate, w_up, b_up, w_down, b_down, tb)
    sorted_e = flat_e[order]
    dest_sorted = pad_start[sorted_e] + (jnp.arange(n_assign, dtype=jnp.int32) - start[sorted_e])
    dest = jnp.zeros((n_assign,), jnp.int32).at[order].set(dest_sorted)
    return jnp.sum(y_rows[dest].reshape(n_tok, TOP_K, d).astype(F32), axis=1)


def kernel(x_prompt, x_sample, cache_mla_latent, cache_mla_krope, cache_fox_k, cache_fox_v, cache_fox_logf,
           page_table, meta_tokens, attn_norm, w_in, b_forget, g_qlat, w_uq, g_kvlat, w_ukv, g_mla_q, g_mla_k,
           g_fox_q, g_fox_k, w_branch_mla, w_branch_fox, w_out, ffn_norm, w_router, b_router, w_gate, b_gate,
           w_up, b_up, w_down, b_down):
    assert attn_norm.shape[0] == 1, "single-layer trunk"
    b, seq, d = x_prompt.shape
    bd, n_q, _ = x_sample.shape
    n_pages = page_table.shape[1]
    n_experts = w_router.shape[-1]
    fw = FOX_HEADS * FOX_DIM
    c_kv = w_ukv.shape[1]

    pw = _proj_weights(attn_norm[0], w_in[0], b_forget[0], g_qlat[0], w_uq[0], g_kvlat[0], w_ukv[0],
                       g_mla_q[0], g_mla_k[0], g_fox_q[0], g_fox_k[0])

    tm = _pick_tile(seq, 512)
    nt = seq // tm
    main = _proj(x_prompt.reshape(b * seq, d), _rope_tables(N_META + jnp.arange(seq)), _tri(tm), pw, b, nt, tm)
    meta_x = jnp.pad(meta_tokens.astype(F32), ((0, LANES - N_META), (0, 0)))
    meta = _proj(meta_x, _rope_tables(jnp.arange(LANES)), _tri(LANES), pw, 1, 1, LANES)
    past = n_pages * PAGE_SIZE
    ns = bd * n_q
    samp = _proj(x_sample.reshape(ns, d), _rope_tables(past + jnp.arange(ns) % n_q), _tri(ns, n_q), pw, 1, 1, ns)

    (ckv_p, kr_p, kfox_p, vfox_p, logf_p, _, nf_p, qm_p, km_p, vme_p, vmo_p, qf_p, kf_p, vfe_p, vfo_p,
     sga_p, sgb_p) = main
    (ckv_m, kr_m, kfox_m, vfox_m, logf_m, _, nf_m, _, km_m, vme_m, vmo_m, _, kf_m, vfe_m, vfo_m, _, _) = meta
    (ckv_s, kr_s, kfox_s, vfox_s, logf_s, fcol_s, _, qm_s, _, _, _, qf_s, _, _, _, sga_s, sgb_s) = samp

    tb = _pick_tile(seq, 256)
    r3 = lambda a: a.reshape(b, seq, a.shape[-1])
    o_mla = _flash(r3(qm_p), r3(km_p), r3(vme_p), r3(vmo_p), km_m, vme_m, vmo_m, None, None, 1, tb)
    nf_meta = nf_m[0] - nf_m[0][:, N_META - 1:N_META]
    nf_blocks = jnp.swapaxes(nf_p.reshape(b, FOX_HEADS, seq // tb, tb), 1, 2)
    o_fox = _flash(r3(qf_p), r3(kf_p), r3(vfe_p), r3(vfo_p), kf_m, vfe_m, vfo_m, nf_blocks, nf_meta, 2, tb)

    pp = _pick_tile(n_pages, 8)
    rows = n_q * MLA_HEADS
    eye_h = jnp.eye(MLA_HEADS, dtype=BF16)
    q4 = qm_s.reshape(bd, n_q, MLA_HEADS, HEAD_SLOT)
    qn_bd = (q4[:, :, :, None, :NOPE_DIM] * eye_h[None, None, :, :, None]).reshape(bd, rows, MLA_HEADS * NOPE_DIM)
    qr = q4[..., NOPE_DIM:QK_DIM].reshape(bd, rows, ROPE_DIM)
    half = ROPE_DIM // 2
    inv_freq = ROPE_THETA ** (-jnp.arange(half, dtype=F32) / half)
    ang = jnp.arange(past, dtype=F32).reshape(n_pages, 1, PAGE_SIZE) * inv_freq[None, :, None]
    ang_new = (past + jnp.minimum(jnp.arange(PAGE_SIZE), n_q - 1)).astype(F32)[None, :] * inv_freq[:, None]
    pad_new = lambda a: jnp.pad(a.reshape(bd, n_q, a.shape[-1]), ((0, 0), (0, PAGE_SIZE - n_q), (0, 0)))
    pad_new_t = lambda a: jnp.pad(jnp.swapaxes(a.reshape(bd, n_q, a.shape[-1]), 1, 2),
                                  ((0, 0), (0, 0), (0, PAGE_SIZE - n_q)))
    w_ukv0 = w_ukv[0]
    wuk_c = w_ukv0[:, :, :NOPE_DIM].reshape(c_kv, MLA_HEADS * NOPE_DIM).astype(BF16)
    o_mla_s = _mla_decode(
        page_table, cache_mla_latent[0], jnp.swapaxes(cache_mla_krope[0], 1, 2), jnp.cos(ang), jnp.sin(ang),
        qn_bd, qr, pad_new(ckv_s), pad_new_t(kr_s), jnp.cos(ang_new), jnp.sin(ang_new),
        jnp.tile(g_mla_k[0][:NOPE_DIM].astype(F32), MLA_HEADS).reshape(1, -1),
        g_mla_k[0][NOPE_DIM:].astype(F32).reshape(ROPE_DIM, 1),
        wuk_c, wuk_c.T, pw["wuv"], pp)

    suf_new = jnp.pad(-jnp.swapaxes(fcol_s.reshape(bd, n_q, FOX_HEADS), 1, 2), ((0, 0), (0, 0), (0, PAGE_SIZE - n_q)))
    qf4 = qf_s.reshape(bd, n_q, FOX_HEADS, 1, HEAD_SLOT)
    pair_of = (jnp.arange(FOX_HEADS)[:, None] // 2 == jnp.arange(FOX_HEADS // 2)[None, :]).astype(BF16)
    qf_bd = (qf4 * pair_of[None, None, :, :, None]).reshape(bd, rows, fw)
    dim_major = lambda c: jnp.transpose(c, (0, 2, 3, 1)).reshape(c.shape[0], fw, PAGE_SIZE)
    o_fox_s = _fox_decode(page_table, dim_major(cache_fox_k[0]), dim_major(cache_fox_v[0]),
                          jnp.swapaxes(cache_fox_logf[0], 1, 2), qf_bd,
                          pad_new_t(kfox_s), pad_new_t(vfox_s), suf_new, pp)

    mw = {
        "wba": w_branch_mla[0].astype(BF16), "wbb": w_branch_fox[0].astype(BF16), "wo": w_out[0].astype(BF16),
        "fn": ffn_norm[0].reshape(1, -1).astype(F32),
        "wr": _pad_cols(w_router[0].astype(F32), LANES), "br": _pad_cols(b_router[0].reshape(1, -1).astype(F32), LANES),
        "n_experts": n_experts,
    }
    h_all = jnp.concatenate([x_prompt.reshape(b * seq, d), x_sample.reshape(ns, d)], axis=0)
    cat = lambda p, s: jnp.concatenate([p.reshape(b * seq, -1), s.reshape(ns, -1)], axis=0)
    n_tok = b * seq + ns
    h2, xn2, idx, gate = _merge(h_all, cat(o_mla, o_mla_s), cat(o_fox, o_fox_s), cat(sga_p, sga_s),
                                cat(sgb_p, sgb_s), mw, _pick_tile(n_tok, 512))

    y = h2 + _moe(xn2, idx[:, :TOP_K], gate[:, :TOP_K], w_gate[0], b_gate[0], w_up[0], b_up[0], w_down[0],
                  b_down[0], _pick_tile(n_tok * TOP_K, 512))
    y_prompt = y[:b * seq].reshape(b, seq, d)
    y_sample = y[b * seq:].reshape(bd, n_q, d)

    def state_p(main_a, meta_a, tail):
        m = jnp.broadcast_to(meta_a[:N_META].reshape((1, N_META) + tail), (b, N_META) + tail)
        return jnp.concatenate([m, main_a.reshape((b, seq) + tail)], axis=1)[None]

    def state_s(a, tail):
        return a.reshape((1, bd, n_q) + tail)

    return (y_prompt, y_sample,
            state_p(ckv_p, ckv_m, (c_kv,)), state_p(kr_p, kr_m, (ROPE_DIM,)),
            state_p(kfox_p, kfox_m, (FOX_HEADS, FOX_DIM)), state_p(vfox_p, vfox_m, (FOX_HEADS, FOX_DIM)),
            state_p(logf_p, logf_m, (FOX_HEADS,)),
            state_s(ckv_s, (c_kv,)), state_s(kr_s, (ROPE_DIM,)),
            state_s(kfox_s, (FOX_HEADS, FOX_DIM)), state_s(vfox_s, (FOX_HEADS, FOX_DIM)),
            state_s(logf_s, (FOX_HEADS,)))
```

```python
import functools

import jax
import jax.numpy as jnp
import numpy as np
from jax import lax
from jax.experimental import pallas as pl
from jax.experimental.pallas import tpu as pltpu

N_META = 16
MLA_HEADS = 8
NOPE_DIM = 64
ROPE_DIM = 32
V_DIM = 64
QK_DIM = NOPE_DIM + ROPE_DIM
ROPE_THETA = 10000.0
MLA_SCALE = QK_DIM ** -0.5
FOX_HEADS = 8
FOX_DIM = 64
FOX_SCALE = FOX_DIM ** -0.5
TOP_K = 4
SWIGLU_ALPHA = 1.702
SWIGLU_LIMIT = 7.0
NORM_EPS = 1e-6
PAGE_SIZE = 128

LANES = 128
VMEM_LIMIT = 56 * 1024 * 1024
HEAD_SLOT = LANES
NEG_INF = float("-inf")

F32 = jnp.float32
BF16 = jnp.bfloat16


def _dot(a, b):
    return jnp.dot(a, b, preferred_element_type=F32)


def _dot_nt(a, b):
    return lax.dot_general(a, b, (((1,), (1,)), ((), ())), preferred_element_type=F32)


def _const_spec(shape):
    n = len(shape)
    return pl.BlockSpec(shape, lambda *_: (0,) * n)


def _rope_lanes(x, c, s1, s2):
    return x * c + pltpu.roll(x, LANES - ROPE_DIM // 2, 1) * s1 + pltpu.roll(x, ROPE_DIM // 2, 1) * s2


def _proj_body(x_ref, c_ref, s1_ref, s2_ref, tri_ref, an_ref, wq_ref, gql_ref, wuq_ref, gq_ref,
               wkv_ref, gkv_ref, wkr0_ref, wkr64_ref, wuk_ref, gk_ref, wuv_ref,
               wfq_ref, gfq_ref, wfk_ref, gfk_ref, wfv_ref, wf3_ref, bf3_ref, wga_ref, wgb_ref,
               ckv_o, kr_o, kfox_o, vfox_o, logf_o, fcol_o, nf_o,
               qm_o, km_o, vme_o, vmo_o, qf_o, kf_o, vfe_o, vfo_o, sga_o, sgb_o,
               carry_scr):
    tm = x_ref.shape[0]
    x = x_ref[...]
    xn = (x * lax.rsqrt(jnp.mean(x * x, axis=-1, keepdims=True) + NORM_EPS) * an_ref[...]).astype(BF16)
    c, s1, s2 = c_ref[...], s1_ref[...], s2_ref[...]
    lane = lax.broadcasted_iota(jnp.int32, (tm, LANES), 1)
    low = lane < FOX_DIM

    ql = _dot(xn, wq_ref[...])
    qn = (ql * lax.rsqrt(jnp.mean(ql * ql, axis=-1, keepdims=True) + NORM_EPS) * gql_ref[...]).astype(BF16)
    q = _dot(qn, wuq_ref[...])
    gq = gq_ref[...]
    for h in range(MLA_HEADS):
        qh = q[:, h * HEAD_SLOT:(h + 1) * HEAD_SLOT]
        r = lax.rsqrt(jnp.sum(qh * qh, axis=-1, keepdims=True) * (1.0 / QK_DIM) + NORM_EPS)
        qm_o[:, h * HEAD_SLOT:(h + 1) * HEAD_SLOT] = _rope_lanes(qh * r * gq, c, s1, s2).astype(BF16)

    kvl = _dot(xn, wkv_ref[...])
    ckv = kvl * lax.rsqrt(jnp.mean(kvl * kvl, axis=-1, keepdims=True) + NORM_EPS) * gkv_ref[...]
    ckv_o[...] = ckv
    c16 = ckv.astype(BF16)
    kr_o[...] = _dot(xn, wkr0_ref[...])[:, :ROPE_DIM]
    kr = _dot(xn, wkr64_ref[...])
    ssq_r = jnp.sum(kr * kr, axis=-1, keepdims=True)
    gk = gk_ref[...]
    kr_rot = _rope_lanes(kr * gk, c, s1, s2)
    kn = _dot(c16, wuk_ref[...])
    for h in range(MLA_HEADS):
        kh = kn[:, h * HEAD_SLOT:(h + 1) * HEAD_SLOT]
        r = lax.rsqrt((jnp.sum(kh * kh, axis=-1, keepdims=True) + ssq_r) * (1.0 / QK_DIM) + NORM_EPS)
        km_o[:, h * HEAD_SLOT:(h + 1) * HEAD_SLOT] = ((kh * gk + kr_rot) * r).astype(BF16)
    vm = _dot(c16, wuv_ref[...])
    for p in range(MLA_HEADS // 2):
        vp = vm[:, p * LANES:(p + 1) * LANES]
        vme_o[:, p * LANES:(p + 1) * LANES] = jnp.where(low, vp, 0.0).astype(BF16)
        vmo_o[:, p * LANES:(p + 1) * LANES] = jnp.where(low, 0.0, vp).astype(BF16)

    def pair_norm(t):
        sq = t * t
        lo = jnp.sum(jnp.where(low, sq, 0.0), axis=-1, keepdims=True)
        hi = jnp.sum(jnp.where(low, 0.0, sq), axis=-1, keepdims=True)
        r_lo = lax.rsqrt(lo * (1.0 / FOX_DIM) + NORM_EPS)
        r_hi = lax.rsqrt(hi * (1.0 / FOX_DIM) + NORM_EPS)
        return t * jnp.where(low, r_lo, r_hi)

    fq = _dot(xn, wfq_ref[...])
    fk = _dot(xn, wfk_ref[...])
    fv = _dot(xn, wfv_ref[...])
    vfox_o[...] = fv
    for p in range(FOX_HEADS // 2):
        sl = slice(p * LANES, (p + 1) * LANES)
        qp = pair_norm(fq[:, sl]) * gfq_ref[:, sl]
        qf_o[:, (2 * p) * HEAD_SLOT:(2 * p + 1) * HEAD_SLOT] = jnp.where(low, qp, 0.0).astype(BF16)
        qf_o[:, (2 * p + 1) * HEAD_SLOT:(2 * p + 2) * HEAD_SLOT] = jnp.where(low, 0.0, qp).astype(BF16)
        kp = pair_norm(fk[:, sl]) * gfk_ref[:, sl]
        kfox_o[:, sl] = kp
        kf_o[:, sl] = kp.astype(BF16)
        vp = fv[:, sl]
        vfe_o[:, sl] = jnp.where(low, vp, 0.0).astype(BF16)
        vfo_o[:, sl] = jnp.where(low, 0.0, vp).astype(BF16)

    fl = _dot(xn, wf3_ref[...]) + bf3_ref[...]
    lf = jnp.minimum(fl, 0.0) - jnp.log(1.0 + jnp.exp(-jnp.abs(fl)))
    logf_o[...] = lf[:, :FOX_HEADS]
    hi = lf.astype(BF16)
    r1 = lf - hi.astype(F32)
    mid = r1.astype(BF16)
    lo = (r1 - mid.astype(F32)).astype(BF16)
    piece = jnp.where(lane < FOX_HEADS, hi.astype(F32), jnp.where(lane < 2 * FOX_HEADS, mid.astype(F32), lo.astype(F32)))
    piece = jnp.where(lane < 3 * FOX_HEADS, piece, 0.0).astype(BF16)
    f3 = _dot(tri_ref[...], piece)
    f = f3 + pltpu.roll(f3, LANES - FOX_HEADS, 1) + pltpu.roll(f3, LANES - 2 * FOX_HEADS, 1)

    @pl.when(pl.program_id(1) == 0)
    def _():
        carry_scr[...] = jnp.zeros_like(carry_scr)

    f = f + carry_scr[0:1, :]
    carry_scr[0:1, :] = f[tm - 1:tm, :]
    fcol_o[...] = f[:, :FOX_HEADS]
    nf_o[0] = jnp.transpose(-f)[:FOX_HEADS, :]

    sga_o[...] = jax.nn.sigmoid(_dot(xn, wga_ref[...])).astype(BF16)
    sgb_o[...] = jax.nn.sigmoid(_dot(xn, wgb_ref[...])).astype(BF16)


def _proj(x, tables, tri, pw, nb, nt, tm):
    rows, d = x.shape
    c_kv = pw["wkv"].shape[1]
    fw = pw["wfq"].shape[1]
    slots = MLA_HEADS * HEAD_SLOT

    def row_spec(width):
        return pl.BlockSpec((tm, width), lambda b, i: (b * nt + i, 0))

    tab_spec = pl.BlockSpec((tm, LANES), lambda b, i: (i, 0))
    weights = [pw[k] for k in ("an", "wq", "gql", "wuq", "gq", "wkv", "gkv", "wkr0", "wkr64", "wuk", "gk",
                               "wuv", "wfq", "gfq", "wfk", "gfk", "wfv", "wf3", "bf3", "wga", "wgb")]
    in_specs = [row_spec(d), tab_spec, tab_spec, tab_spec, _const_spec(tri.shape)]
    in_specs += [_const_spec(w.shape) for w in weights]
    out_shapes = [
        jax.ShapeDtypeStruct((rows, c_kv), F32),
        jax.ShapeDtypeStruct((rows, ROPE_DIM), F32),
        jax.ShapeDtypeStruct((rows, fw), F32),
        jax.ShapeDtypeStruct((rows, fw), F32),
        jax.ShapeDtypeStruct((rows, FOX_HEADS), F32),
        jax.ShapeDtypeStruct((rows, FOX_HEADS), F32),
        jax.ShapeDtypeStruct((nb, FOX_HEADS, nt * tm), F32),
        jax.ShapeDtypeStruct((rows, slots), BF16),
        jax.ShapeDtypeStruct((rows, slots), BF16),
        jax.ShapeDtypeStruct((rows, fw), BF16),
        jax.ShapeDtypeStruct((rows, fw), BF16),
        jax.ShapeDtypeStruct((rows, slots), BF16),
        jax.ShapeDtypeStruct((rows, fw), BF16),
        jax.ShapeDtypeStruct((rows, fw), BF16),
        jax.ShapeDtypeStruct((rows, fw), BF16),
        jax.ShapeDtypeStruct((rows, d), BF16),
        jax.ShapeDtypeStruct((rows, d), BF16),
    ]
    out_specs = [row_spec(s.shape[1]) for s in out_shapes]
    out_specs[6] = pl.BlockSpec((1, FOX_HEADS, tm), lambda b, i: (b, 0, i))
    return pl.pallas_call(
        _proj_body,
        grid=(nb, nt),
        in_specs=in_specs,
        out_specs=out_specs,
        out_shape=out_shapes,
        scratch_shapes=[pltpu.VMEM((8, LANES), F32)],
        compiler_params=pltpu.CompilerParams(dimension_semantics=("arbitrary", "arbitrary"),
                                             vmem_limit_bytes=VMEM_LIMIT),
        name="proj",
    )(x, *tables, tri, *weights)


def _flash_body(q_ref, k_ref, ve_ref, vo_ref, km_ref, vem_ref, vom_ref, *rest, k_per_slot, has_bias, tb):
    if has_bias:
        nf_ref, nfm_ref, o_ref, m_scr, l_scr, acc_scr = rest
    else:
        o_ref, m_scr, l_scr, acc_scr = rest
    qi = pl.program_id(1)
    n_meta = km_ref.shape[0]

    def step(keys, values, bias, mask):
        scores = []
        for h in range(MLA_HEADS):
            s = _dot_nt(q_ref[0, :, h * HEAD_SLOT:(h + 1) * HEAD_SLOT], keys(h))
            if bias is not None:
                s = s + bias(h)
            scores.append(s if mask is None else jnp.where(mask, s, NEG_INF))
        probs = []
        for h in range(MLA_HEADS):
            s = scores[h]
            m_old = m_scr[h]
            m_new = jnp.maximum(m_old, jnp.max(s, axis=-1, keepdims=True))
            alpha = jnp.exp(m_old - m_new)
            tiles = [jnp.exp(s[:, c * LANES:(c + 1) * LANES] - m_new) for c in range(s.shape[1] // LANES)]
            l_scr[h] = l_scr[h] * alpha + functools.reduce(lambda a, b_: a + b_, tiles)
            m_scr[h] = m_new
            acc_scr[h] = acc_scr[h] * alpha
            probs.append(jnp.concatenate(tiles, axis=1).astype(BF16) if len(tiles) > 1 else tiles[0].astype(BF16))
        for h in range(MLA_HEADS):
            acc_scr[h] = acc_scr[h] + _dot(probs[h], values(h))

    def slots(h):
        return (h // k_per_slot) * LANES, (h // 2) * LANES, h % 2 == 0

    m_scr[...] = jnp.full_like(m_scr, NEG_INF)
    l_scr[...] = jnp.zeros_like(l_scr)
    acc_scr[...] = jnp.zeros_like(acc_scr)
    step(lambda h: km_ref[:, slots(h)[0]:slots(h)[0] + LANES],
         lambda h: (vem_ref if slots(h)[2] else vom_ref)[:, slots(h)[1]:slots(h)[1] + LANES],
         (lambda h: nfm_ref[h:h + 1, :]) if has_bias else None,
         lax.broadcasted_iota(jnp.int32, (tb, n_meta), 1) < N_META)

    def block(j, mask):
        start = pl.multiple_of(j * tb, tb)
        step(lambda h: k_ref[0, pl.ds(start, tb), slots(h)[0]:slots(h)[0] + LANES],
             lambda h: (ve_ref if slots(h)[2] else vo_ref)[0, pl.ds(start, tb), slots(h)[1]:slots(h)[1] + LANES],
             (lambda h: nf_ref[0, j, h:h + 1, :]) if has_bias else None, mask)

    def full_block(j, carry):
        block(j, None)
        return carry

    lax.fori_loop(0, qi, full_block, 0)
    block(qi, lax.broadcasted_iota(jnp.int32, (tb, tb), 1) <= lax.broadcasted_iota(jnp.int32, (tb, tb), 0))
    for p in range(MLA_HEADS // 2):
        inv = [1.0 / jnp.sum(l_scr[h], axis=-1, keepdims=True) for h in (2 * p, 2 * p + 1)]
        out = acc_scr[2 * p] * inv[0] + acc_scr[2 * p + 1] * inv[1]
        o_ref[0, :, p * LANES:(p + 1) * LANES] = out.astype(o_ref.dtype)


def _flash(q, k, ve, vo, km, vem, vom, nf, nfm, k_per_slot, tb):
    b, t, _ = q.shape
    has_bias = nf is not None
    nq = t // tb
    in_specs = [
        pl.BlockSpec((1, tb, q.shape[2]), lambda bi, i: (bi, i, 0)),
        pl.BlockSpec((1, t, k.shape[2]), lambda bi, i: (bi, 0, 0)),
        pl.BlockSpec((1, t, ve.shape[2]), lambda bi, i: (bi, 0, 0)),
        pl.BlockSpec((1, t, vo.shape[2]), lambda bi, i: (bi, 0, 0)),
        _const_spec(km.shape), _const_spec(vem.shape), _const_spec(vom.shape),
    ]
    args = [q, k, ve, vo, km, vem, vom]
    if has_bias:
        in_specs += [pl.BlockSpec((1, nq, FOX_HEADS, tb), lambda bi, i: (bi, 0, 0, 0)), _const_spec(nfm.shape)]
        args += [nf, nfm]
    return pl.pallas_call(
        functools.partial(_flash_body, k_per_slot=k_per_slot, has_bias=has_bias, tb=tb),
        grid=(b, nq),
        in_specs=in_specs,
        out_specs=pl.BlockSpec((1, tb, ve.shape[2]), lambda bi, i: (bi, i, 0)),
        out_shape=jax.ShapeDtypeStruct((b, t, ve.shape[2]), BF16),
        scratch_shapes=[pltpu.VMEM((MLA_HEADS, tb, LANES), F32)] * 3,
        compiler_params=pltpu.CompilerParams(dimension_semantics=("arbitrary", "arbitrary"),
                                             vmem_limit_bytes=VMEM_LIMIT),
        name="flash_fox" if has_bias else "flash_mla",
    )(*args)


def _softmax_step(s, m_scr, l_scr):
    m_old = m_scr[...]
    m_new = jnp.maximum(m_old, jnp.max(s, axis=-1, keepdims=True))
    alpha = jnp.exp(m_old - m_new)
    p = jnp.exp(s - m_new)
    l_scr[...] = l_scr[...] * alpha + jnp.sum(p, axis=-1, keepdims=True)
    m_scr[...] = m_new
    return p, alpha


def _new_token_mask(rows, n_new):
    t = lax.broadcasted_iota(jnp.int32, (rows, PAGE_SIZE), 1)
    qidx = lax.broadcasted_iota(jnp.int32, (rows, PAGE_SIZE), 0) // MLA_HEADS
    return (t <= qidx) & (t < n_new)


def _fold_heads(full, n_q, width):
    rows = full.shape[0]
    lane_head = lax.broadcasted_iota(jnp.int32, (rows, full.shape[1]), 1) // width
    row_head = lax.broadcasted_iota(jnp.int32, (rows, full.shape[1]), 0) % MLA_HEADS
    kept = jnp.where(lane_head == row_head, full, 0.0)
    return jnp.sum(kept.reshape(n_q, MLA_HEADS, full.shape[1]), axis=1)


def _mla_dec_body(pt_ref, *refs, pp, n_q):
    lat_refs = refs[:pp]
    kr_refs = refs[pp:2 * pp]
    (cos_ref, sin_ref, qn_ref, qr_ref, latn_ref, krn_ref, cosn_ref, sinn_ref,
     gkn_ref, gkr_ref, wuk_ref, wukt_ref, wuv_ref, o_ref, m_scr, l_scr, acc_scr, lhs_scr) = refs[2 * pp:]
    g = pl.program_id(1)
    rows = n_q * MLA_HEADS
    half = ROPE_DIM // 2
    n_up = wukt_ref.shape[0]

    @pl.when(g == 0)
    def _():
        m_scr[...] = jnp.full_like(m_scr, NEG_INF)
        l_scr[...] = jnp.zeros_like(l_scr)
        acc_scr[...] = jnp.zeros_like(acc_scr)
        lhs_scr[:n_up, :] = wukt_ref[...]
        qn = (qn_ref[0].astype(F32) * gkn_ref[...]).astype(BF16)
        lhs_scr[n_up:, :] = _dot_nt(qn, wuk_ref[...]).astype(BF16)

    def chunk(c16, krt, cos_t, sin_t, mask):
        t = c16.shape[0]
        big = _dot_nt(lhs_scr[...], c16)
        knt = big[:n_up]
        ssq_n = jnp.sum((knt * knt).reshape(MLA_HEADS, NOPE_DIM, t), axis=1)
        ssq_r = jnp.sum(krt * krt, axis=0, keepdims=True)
        r = lax.rsqrt((ssq_n + ssq_r) * (1.0 / QK_DIM) + NORM_EPS)
        kg = krt * gkr_ref[...]
        x1, x2 = kg[:half], kg[half:]
        k_rot = jnp.concatenate([x1 * cos_t - x2 * sin_t, x2 * cos_t + x1 * sin_t], axis=0).astype(BF16)
        s = (big[n_up:] + _dot(qr_ref[0], k_rot)) * jnp.concatenate([r] * n_q, axis=0)
        if mask is not None:
            s = jnp.where(mask, s, NEG_INF)
        p, alpha = _softmax_step(s, m_scr, l_scr)
        acc_scr[...] = acc_scr[...] * alpha + _dot(p.astype(BF16), c16)

    chunk(jnp.concatenate([lat_refs[i][0].astype(BF16) for i in range(pp)], axis=0),
          jnp.concatenate([kr_refs[i][0] for i in range(pp)], axis=1), cos_ref[0], sin_ref[0], None)

    @pl.when(g == pl.num_programs(1) - 1)
    def _():
        chunk(latn_ref[0].astype(BF16), krn_ref[0], cosn_ref[...], sinn_ref[...], _new_token_mask(rows, n_q))
        o_lat = (acc_scr[...] * (1.0 / l_scr[...])).astype(BF16)
        o_ref[0] = _fold_heads(_dot(o_lat, wuv_ref[...]), n_q, V_DIM).astype(o_ref.dtype)


def _mla_decode(page_table, cache_lat, cache_kr, cos_t, sin_t, qn_bd, qr, lat_new, kr_new, cos_new, sin_new,
                gkn, gkr, wuk, wukt, wuv, pp):
    bd, n_pages = page_table.shape
    n_q = qr.shape[1] // MLA_HEADS
    rows = qr.shape[1]
    c_kv = cache_lat.shape[-1]

    def page_spec(shape, i):
        return pl.BlockSpec((1,) + shape, lambda b, g, pt: (pt[b, g * pp + i], 0, 0))

    def batch_spec(shape):
        return pl.BlockSpec((1,) + shape, lambda b, g, pt: (b, 0, 0))

    def const_spec(shape):
        n = len(shape)
        return pl.BlockSpec(shape, lambda b, g, pt: (0,) * n)

    in_specs = [page_spec((PAGE_SIZE, c_kv), i) for i in range(pp)]
    in_specs += [page_spec((ROPE_DIM, PAGE_SIZE), i) for i in range(pp)]
    in_specs += [
        pl.BlockSpec((1, ROPE_DIM // 2, pp * PAGE_SIZE), lambda b, g, pt: (g, 0, 0)),
        pl.BlockSpec((1, ROPE_DIM // 2, pp * PAGE_SIZE), lambda b, g, pt: (g, 0, 0)),
        batch_spec(qn_bd.shape[1:]), batch_spec(qr.shape[1:]),
        batch_spec(lat_new.shape[1:]), batch_spec(kr_new.shape[1:]),
        const_spec(cos_new.shape), const_spec(sin_new.shape),
        const_spec(gkn.shape), const_spec(gkr.shape), const_spec(wuk.shape), const_spec(wukt.shape),
        const_spec(wuv.shape),
    ]
    grid_spec = pltpu.PrefetchScalarGridSpec(
        num_scalar_prefetch=1,
        grid=(bd, n_pages // pp),
        in_specs=in_specs,
        out_specs=pl.BlockSpec((1, n_q, wuv.shape[1]), lambda b, g, pt: (b, 0, 0)),
        scratch_shapes=[pltpu.VMEM((rows, 1), F32), pltpu.VMEM((rows, 1), F32),
                        pltpu.VMEM((rows, c_kv), F32), pltpu.VMEM((wukt.shape[0] + rows, c_kv), BF16)],
    )
    return pl.pallas_call(
        functools.partial(_mla_dec_body, pp=pp, n_q=n_q),
        grid_spec=grid_spec,
        out_shape=jax.ShapeDtypeStruct((bd, n_q, wuv.shape[1]), BF16),
        compiler_params=pltpu.CompilerParams(dimension_semantics=("arbitrary", "arbitrary"),
                                             vmem_limit_bytes=VMEM_LIMIT),
        name="mla_decode",
    )(page_table, *([cache_lat] * pp), *([cache_kr] * pp), cos_t, sin_t, qn_bd, qr, lat_new, kr_new,
      cos_new, sin_new, gkn, gkr, wuk, wukt, wuv)


def _fox_dec_body(pt_ref, *refs, pp, n_q):
    k_refs = refs[:pp]
    v_refs = refs[pp:2 * pp]
    lf_refs = refs[2 * pp:3 * pp]
    us_ref, q_ref, kn_ref, vn_ref, sufn_ref, o_ref, m_scr, l_scr, acc_scr, suf_scr = refs[3 * pp:]
    g = pl.program_id(1)
    rows = n_q * FOX_HEADS

    def chunk(kt16, vt16, bias, mask):
        s = _dot(q_ref[0], kt16) + jnp.concatenate([bias] * n_q, axis=0)
        if mask is not None:
            s = jnp.where(mask, s, NEG_INF)
        p, alpha = _softmax_step(s, m_scr, l_scr)
        acc_scr[...] = acc_scr[...] * alpha + _dot_nt(p.astype(BF16), vt16)

    @pl.when(g == 0)
    def _():
        m_scr[...] = jnp.full_like(m_scr, NEG_INF)
        l_scr[...] = jnp.zeros_like(l_scr)
        acc_scr[...] = jnp.zeros_like(acc_scr)
        suf_scr[...] = jnp.zeros_like(suf_scr)
        chunk(kn_ref[0].astype(BF16), vn_ref[0].astype(BF16), sufn_ref[0], _new_token_mask(rows, n_q))

    lfs = [lf_refs[i][0] for i in range(pp)]
    pieces = []
    for lf in lfs:
        hi = lf.astype(BF16).astype(F32)
        mid = (lf - hi).astype(BF16).astype(F32)
        pieces += [hi, mid, lf - hi - mid]
    w = _dot(jnp.concatenate(pieces, axis=0).astype(BF16), us_ref[...])
    carry = suf_scr[...]
    biases = [None] * pp
    for i in reversed(range(pp)):
        base = 3 * FOX_HEADS * i
        within = (w[base:base + FOX_HEADS] + w[base + FOX_HEADS:base + 2 * FOX_HEADS]
                  + w[base + 2 * FOX_HEADS:base + 3 * FOX_HEADS])
        biases[i] = within + carry
        carry = carry + jnp.sum(lfs[i], axis=1, keepdims=True)
    suf_scr[...] = carry
    chunk(jnp.concatenate([k_refs[i][0].astype(BF16) for i in range(pp)], axis=1),
          jnp.concatenate([v_refs[i][0].astype(BF16) for i in range(pp)], axis=1),
          jnp.concatenate(biases, axis=1), None)

    @pl.when(g == pl.num_programs(1) - 1)
    def _():
        out = acc_scr[...] * (1.0 / l_scr[...])
        o_ref[0] = _fold_heads(out, n_q, FOX_DIM).astype(o_ref.dtype)


def _fox_decode(page_table, cache_kt, cache_vt, cache_lf, q_bd, kt_new, vt_new, suf_new, pp):
    bd, n_pages = page_table.shape
    rows = q_bd.shape[1]
    n_q = rows // FOX_HEADS
    fw = cache_kt.shape[1]
    ng = n_pages // pp
    i_tok = np.arange(PAGE_SIZE)
    u_strict = jnp.asarray(i_tok[:, None] > i_tok[None, :], dtype=BF16)

    def page_spec(shape, i):
        return pl.BlockSpec((1,) + shape, lambda b, g, pt: (pt[b, (ng - 1 - g) * pp + i], 0, 0))

    def batch_spec(shape):
        return pl.BlockSpec((1,) + shape, lambda b, g, pt: (b,) + (0,) * len(shape))

    in_specs = [page_spec((fw, PAGE_SIZE), i) for i in range(pp)]
    in_specs += [page_spec((fw, PAGE_SIZE), i) for i in range(pp)]
    in_specs += [page_spec((FOX_HEADS, PAGE_SIZE), i) for i in range(pp)]
    in_specs += [
        pl.BlockSpec(u_strict.shape, lambda b, g, pt: (0, 0)),
        batch_spec(q_bd.shape[1:]), batch_spec(kt_new.shape[1:]), batch_spec(vt_new.shape[1:]),
        batch_spec(suf_new.shape[1:]),
    ]
    grid_spec = pltpu.PrefetchScalarGridSpec(
        num_scalar_prefetch=1,
        grid=(bd, ng),
        in_specs=in_specs,
        out_specs=pl.BlockSpec((1, n_q, fw), lambda b, g, pt: (b, 0, 0)),
        scratch_shapes=[pltpu.VMEM((rows, 1), F32), pltpu.VMEM((rows, 1), F32), pltpu.VMEM((rows, fw), F32),
                        pltpu.VMEM((FOX_HEADS, PAGE_SIZE), F32)],
    )
    return pl.pallas_call(
        functools.partial(_fox_dec_body, pp=pp, n_q=n_q),
        grid_spec=grid_spec,
        out_shape=jax.ShapeDtypeStruct((bd, n_q, fw), BF16),
        compiler_params=pltpu.CompilerParams(dimension_semantics=("arbitrary", "arbitrary"),
                                             vmem_limit_bytes=VMEM_LIMIT),
        name="fox_decode",
    )(page_table, *([cache_kt] * pp), *([cache_vt] * pp), *([cache_lf] * pp), u_strict, q_bd, kt_new, vt_new,
      suf_new)


ROUTE_IDX, ROUTE_RANK, ROUTE_GATE = 0, TOP_K, 2 * TOP_K


def _lane_pick(x, lane, which):
    return jnp.sum(jnp.where(lane == which, x, 0.0), axis=-1, keepdims=True)


def _merge_body(*refs, n_experts, n_aliased):
    (h_ref, oa_ref, ob_ref, sga_ref, sgb_ref, tri_ref, wba_ref, wbb_ref, wo_ref, fn_ref, wr_ref, br_ref,
     h2_o, xn_o, route_o, cnt_o) = refs[n_aliased:]
    tm = h_ref.shape[0]
    ua = _dot(oa_ref[...], wba_ref[...]) * sga_ref[...].astype(F32)
    ub = _dot(ob_ref[...], wbb_ref[...]) * sgb_ref[...].astype(F32)
    h2 = h_ref[...] + _dot((ua + ub).astype(BF16), wo_ref[...])
    h2_o[...] = h2
    xn = h2 * lax.rsqrt(jnp.mean(h2 * h2, axis=-1, keepdims=True) + NORM_EPS) * fn_ref[...]
    xn_o[...] = xn.astype(BF16)
    logits = jnp.dot(xn, wr_ref[...], precision=lax.Precision.HIGHEST, preferred_element_type=F32) + br_ref[...]
    lane = lax.broadcasted_iota(jnp.int32, (tm, LANES), 1).astype(F32)
    logits = jnp.where(lane < n_experts, logits, NEG_INF)
    route = jnp.zeros((tm, LANES), F32)
    vals = jnp.full((tm, LANES), NEG_INF, F32)
    onehot = jnp.zeros((tm, LANES), F32)
    args = []
    for k in range(TOP_K):
        top = jnp.max(logits, axis=-1, keepdims=True)
        arg = jnp.min(jnp.where(logits == top, lane, float(LANES)), axis=-1, keepdims=True)
        route = jnp.where(lane == ROUTE_IDX + k, arg, route)
        vals = jnp.where(lane == k, top, vals)
        onehot = onehot + jnp.where(lane == arg, 1.0, 0.0)
        logits = jnp.where(lane == arg, NEG_INF, logits)
        args.append(arg)
    e = jnp.exp(vals - jnp.max(vals, axis=-1, keepdims=True))
    gates = e / jnp.sum(e, axis=-1, keepdims=True)
    earlier = _dot(tri_ref[...], onehot.astype(BF16))
    for k in range(TOP_K):
        route = jnp.where(lane == ROUTE_RANK + k, _lane_pick(earlier, lane, args[k]), route)
    in_gate = (lane >= ROUTE_GATE) & (lane < ROUTE_GATE + TOP_K)
    route_o[...] = jnp.where(in_gate, pltpu.roll(gates, ROUTE_GATE, 1), route)
    cnt_o[0] = jnp.broadcast_to(jnp.sum(onehot, axis=0, keepdims=True), cnt_o.shape[1:])


def _merge(h, oa, ob, sga, sgb, mw, tm, n_rows_total, tile_offset, prev):
    rows, d = h.shape
    n_experts = mw["n_experts"]
    n_tiles_total = n_rows_total // tm
    i = np.arange(tm)
    tri_strict = jnp.asarray(i[None, :] < i[:, None], dtype=BF16)

    def row_spec(width):
        return pl.BlockSpec((tm, width), lambda i: (i, 0))

    def out_spec(width):
        return pl.BlockSpec((tm, width), lambda i: (i + tile_offset, 0))

    weights = [mw[k] for k in ("wba", "wbb", "wo", "fn", "wr", "br")]
    n_aliased = 0 if prev is None else len(prev)
    aliased = [] if prev is None else list(prev)
    return pl.pallas_call(
        functools.partial(_merge_body, n_experts=n_experts, n_aliased=n_aliased),
        grid=(rows // tm,),
        in_specs=[pl.BlockSpec(memory_space=pl.ANY)] * n_aliased
        + [row_spec(d), row_spec(oa.shape[1]), row_spec(ob.shape[1]), row_spec(d), row_spec(d),
           _const_spec(tri_strict.shape)] + [_const_spec(w.shape) for w in weights],
        out_specs=[out_spec(d), out_spec(d), out_spec(LANES),
                   pl.BlockSpec((1, 8, LANES), lambda i: (i + tile_offset, 0, 0))],
        out_shape=[jax.ShapeDtypeStruct((n_rows_total, d), F32), jax.ShapeDtypeStruct((n_rows_total, d), BF16),
                   jax.ShapeDtypeStruct((n_rows_total, LANES), F32),
                   jax.ShapeDtypeStruct((n_tiles_total, 8, LANES), F32)],
        input_output_aliases={k: k for k in range(n_aliased)},
        compiler_params=pltpu.CompilerParams(dimension_semantics=("arbitrary",), vmem_limit_bytes=VMEM_LIMIT),
        name="merge",
    )(*aliased, h, oa, ob, sga, sgb, tri_strict, *weights)


def _dest_body(route_ref, base_ref, dest_o):
    tm = route_ref.shape[0]
    lane = lax.broadcasted_iota(jnp.int32, (tm, LANES), 1).astype(F32)
    route = route_ref[...]
    base = base_ref[0, 0:1, :]
    out = jnp.zeros((tm, LANES), F32)
    for k in range(TOP_K):
        e_k = _lane_pick(route, lane, float(ROUTE_IDX + k))
        row = _lane_pick(base, lane, e_k) + _lane_pick(route, lane, float(ROUTE_RANK + k))
        out = jnp.where(lane == k, row, out)
    dest_o[...] = out.astype(jnp.int32)


def _dest(route, base, tm):
    rows = route.shape[0]
    return pl.pallas_call(
        _dest_body,
        grid=(rows // tm,),
        in_specs=[pl.BlockSpec((tm, LANES), lambda i: (i, 0)), pl.BlockSpec((1, 8, LANES), lambda i: (i, 0, 0))],
        out_specs=pl.BlockSpec((tm, LANES), lambda i: (i, 0)),
        out_shape=jax.ShapeDtypeStruct((rows, LANES), jnp.int32),
        compiler_params=pltpu.CompilerParams(dimension_semantics=("arbitrary",)),
        name="dest",
    )(route, base)


def _combine_body(*refs):
    g_refs = refs[:TOP_K]
    route_ref, h2_ref, y_o = refs[TOP_K:]
    tm = h2_ref.shape[0]
    lane = lax.broadcasted_iota(jnp.int32, (tm, LANES), 1).astype(F32)
    route = route_ref[...]
    y = h2_ref[...]
    for k in range(TOP_K):
        y = y + _lane_pick(route, lane, float(ROUTE_GATE + k)) * g_refs[k][...].astype(F32)
    y_o[...] = y


def _combine(g, route, h2, tm, tile_offset, n_tiles):
    n_tok, d = h2.shape
    tiles_total = n_tok // tm
    spec = lambda width: pl.BlockSpec((tm, width), lambda i: (i + tile_offset, 0))
    g_specs = [pl.BlockSpec((tm, d), functools.partial(lambda i, k: (i + tile_offset + k * tiles_total, 0), k=k))
               for k in range(TOP_K)]
    return pl.pallas_call(
        _combine_body,
        grid=(n_tiles,),
        in_specs=g_specs + [spec(LANES), spec(d)],
        out_specs=pl.BlockSpec((tm, d), lambda i: (i, 0)),
        out_shape=jax.ShapeDtypeStruct((n_tiles * tm, d), F32),
        compiler_params=pltpu.CompilerParams(dimension_semantics=("arbitrary",), vmem_limit_bytes=VMEM_LIMIT),
        name="combine",
    )(*([g] * TOP_K), route, h2)


def _ffn_body(be_ref, nu_ref, x_ref, wg_ref, bg_ref, wu_ref, bu_ref, wd_ref, bd_ref, o_ref):
    @pl.when(pl.program_id(0) < nu_ref[0])
    def _():
        x = x_ref[...]
        g = _dot(x, wg_ref[0].astype(BF16)) + bg_ref[0]
        u = _dot(x, wu_ref[0].astype(BF16)) + bu_ref[0]
        g = jnp.minimum(g, SWIGLU_LIMIT)
        u = jnp.clip(u, -SWIGLU_LIMIT, SWIGLU_LIMIT)
        hid = ((u + 1.0) * g * jax.nn.sigmoid(SWIGLU_ALPHA * g)).astype(BF16)
        o_ref[...] = (_dot(hid, wd_ref[0].astype(BF16)) + bd_ref[0]).astype(o_ref.dtype)


def _ffn(block_e, n_used, x_rows, w_gate, b_gate, w_up, b_up, w_down, b_down, tb):
    n_rows, d = x_rows.shape
    e, _, f = w_gate.shape

    def wspec(shape):
        return pl.BlockSpec((1,) + shape, lambda i, be, nu: (be[i], 0, 0))

    grid_spec = pltpu.PrefetchScalarGridSpec(
        num_scalar_prefetch=2,
        grid=(n_rows // tb,),
        in_specs=[pl.BlockSpec((tb, d), lambda i, be, nu: (i, 0)),
                  wspec((d, f)), wspec((1, f)), wspec((d, f)), wspec((1, f)), wspec((f, d)), wspec((1, d))],
        out_specs=pl.BlockSpec((tb, d), lambda i, be, nu: (i, 0)),
    )
    return pl.pallas_call(
        _ffn_body,
        grid_spec=grid_spec,
        out_shape=jax.ShapeDtypeStruct((n_rows, d), BF16),
        compiler_params=pltpu.CompilerParams(dimension_semantics=("arbitrary",), vmem_limit_bytes=VMEM_LIMIT),
        name="ffn",
    )(block_e, n_used, x_rows, w_gate, b_gate.reshape(e, 1, f), w_up, b_up.reshape(e, 1, f),
      w_down, b_down.reshape(e, 1, d))


def _rope_tables(pos):
    half = ROPE_DIM // 2
    inv_freq = ROPE_THETA ** (-jnp.arange(half, dtype=F32) / half)
    ang = pos.astype(F32)[:, None] * inv_freq[None, :]
    cos, sin = jnp.cos(ang), jnp.sin(ang)
    n = pos.shape[0]
    zeros = jnp.zeros((n, half), F32)
    c = jnp.concatenate([jnp.ones((n, NOPE_DIM), F32), cos, cos, jnp.ones((n, LANES - QK_DIM), F32)], axis=1)
    s1 = jnp.concatenate([jnp.zeros((n, NOPE_DIM), F32), -sin, zeros, jnp.zeros((n, LANES - QK_DIM), F32)], axis=1)
    s2 = jnp.concatenate([jnp.zeros((n, NOPE_DIM), F32), zeros, sin, jnp.zeros((n, LANES - QK_DIM), F32)], axis=1)
    return c, s1, s2


def _pad_cols(w, width, offset=0):
    return jnp.pad(w, ((0, 0), (offset, width - offset - w.shape[1])))


def _proj_weights(attn_norm, w_in, b_forget, g_qlat, w_uq, g_kvlat, w_ukv, g_mla_q, g_mla_k, g_fox_q, g_fox_k):
    q_lora, c_kv = w_uq.shape[0], w_ukv.shape[0]
    fw = FOX_HEADS * FOX_DIM
    d = w_in.shape[0]
    sizes = (q_lora, c_kv, ROPE_DIM, fw, fw, fw, FOX_HEADS, d, d)
    offs = np.concatenate([[0], np.cumsum(sizes)])
    wq, wkv, wkr, wfq, wfk, wfv, wf, wga, wgb = [w_in[:, offs[i]:offs[i + 1]] for i in range(len(sizes))]
    pad_head = LANES - QK_DIM
    wuq = jnp.pad(w_uq, ((0, 0), (0, 0), (0, pad_head))).reshape(q_lora, MLA_HEADS * HEAD_SLOT)
    wuk = jnp.pad(w_ukv[:, :, :NOPE_DIM], ((0, 0), (0, 0), (0, LANES - NOPE_DIM))).reshape(c_kv, MLA_HEADS * HEAD_SLOT)
    wuv = w_ukv[:, :, NOPE_DIM:].reshape(c_kv, MLA_HEADS * V_DIM)
    row = lambda v: v.reshape(1, -1).astype(F32)
    return {
        "an": row(attn_norm), "wq": wq.astype(BF16), "gql": row(g_qlat), "wuq": wuq.astype(BF16),
        "gq": _pad_cols(row(g_mla_q) * MLA_SCALE, LANES),
        "wkv": wkv.astype(BF16), "gkv": row(g_kvlat),
        "wkr0": _pad_cols(wkr, LANES).astype(BF16), "wkr64": _pad_cols(wkr, LANES, NOPE_DIM).astype(BF16),
        "wuk": wuk.astype(BF16), "gk": _pad_cols(row(g_mla_k), LANES), "wuv": wuv.astype(BF16),
        "wfq": wfq.astype(BF16), "gfq": jnp.tile(row(g_fox_q), (1, FOX_HEADS)) * FOX_SCALE,
        "wfk": wfk.astype(BF16), "gfk": jnp.tile(row(g_fox_k), (1, FOX_HEADS)),
        "wfv": wfv.astype(BF16),
        "wf3": _pad_cols(jnp.tile(wf, (1, 3)), LANES).astype(BF16),
        "bf3": _pad_cols(jnp.tile(row(b_forget), (1, 3)), LANES),
        "wga": wga.astype(BF16), "wgb": wgb.astype(BF16),
    }


def _tri(n, group=None):
    i = np.arange(n)
    m = i[None, :] <= i[:, None]
    if group is not None:
        m &= (i[None, :] // group) == (i[:, None] // group)
    return jnp.asarray(m, dtype=BF16)


def _pick_tile(n, target):
    t = min(n, target)
    while n % t:
        t //= 2
    return t


def _moe_rows(xn, route, cnt, w_gate, b_gate, w_up, b_up, w_down, b_down, tm, tb):
    n_tok, d = xn.shape
    n_experts = w_gate.shape[0]
    n_assign = n_tok * TOP_K
    n_tiles = n_tok // tm
    cnt = cnt[:, 0, :n_experts].astype(jnp.int32)
    counts = jnp.sum(cnt, axis=0)
    tile_off = jnp.cumsum(cnt, axis=0) - cnt
    padded = (counts + tb - 1) // tb * tb
    pad_end = jnp.cumsum(padded)
    pad_start = pad_end - padded
    start = jnp.cumsum(counts) - counts
    n_blocks = -(-n_assign // tb) + n_experts
    n_used = (pad_end[-1] // tb).astype(jnp.int32).reshape(1)
    block_start = jnp.arange(n_blocks, dtype=jnp.int32) * tb
    block_e = jnp.minimum(jnp.sum((pad_end[None, :] <= block_start[:, None]).astype(jnp.int32), axis=1),
                          n_experts - 1)
    base = jnp.zeros((n_tiles, 8, LANES), F32).at[:, 0, :n_experts].set((pad_start[None, :] + tile_off).astype(F32))
    dest = _dest(route, base, tm)[:, :TOP_K].T.reshape(-1)
    flat_e = route[:, ROUTE_IDX:ROUTE_IDX + TOP_K].astype(jnp.int32).reshape(-1)
    order = jnp.argsort(flat_e)
    delta = start[block_e] - pad_start[block_e]
    pos = jnp.arange(n_blocks * tb, dtype=jnp.int32).reshape(n_blocks, tb) + delta[:, None]
    src_tok = order[jnp.clip(pos, 0, n_assign - 1)] // TOP_K
    x_rows = xn[src_tok.reshape(-1)]
    y_rows = _ffn(block_e.astype(jnp.int32), n_used, x_rows, w_gate, b_gate, w_up, b_up, w_down, b_down, tb)
    return y_rows[dest]


def kernel(x_prompt, x_sample, cache_mla_latent, cache_mla_krope, cache_fox_k, cache_fox_v, cache_fox_logf,
           page_table, meta_tokens, attn_norm, w_in, b_forget, g_qlat, w_uq, g_kvlat, w_ukv, g_mla_q, g_mla_k,
           g_fox_q, g_fox_k, w_branch_mla, w_branch_fox, w_out, ffn_norm, w_router, b_router, w_gate, b_gate,
           w_up, b_up, w_down, b_down):
    assert attn_norm.shape[0] == 1, "single-layer trunk"
    b, seq, d = x_prompt.shape
    bd, n_q, _ = x_sample.shape
    n_pages = page_table.shape[1]
    n_experts = w_router.shape[-1]
    fw = FOX_HEADS * FOX_DIM
    c_kv = w_ukv.shape[1]

    pw = _proj_weights(attn_norm[0], w_in[0], b_forget[0], g_qlat[0], w_uq[0], g_kvlat[0], w_ukv[0],
                       g_mla_q[0], g_mla_k[0], g_fox_q[0], g_fox_k[0])

    tm = _pick_tile(seq, 512)
    nt = seq // tm
    main = _proj(x_prompt.reshape(b * seq, d), _rope_tables(N_META + jnp.arange(seq)), _tri(tm), pw, b, nt, tm)
    meta_x = jnp.pad(meta_tokens.astype(F32), ((0, LANES - N_META), (0, 0)))
    meta = _proj(meta_x, _rope_tables(jnp.arange(LANES)), _tri(LANES), pw, 1, 1, LANES)
    past = n_pages * PAGE_SIZE
    ns = bd * n_q
    samp = _proj(x_sample.reshape(ns, d), _rope_tables(past + jnp.arange(ns) % n_q), _tri(ns, n_q), pw, 1, 1, ns)

    (ckv_p, kr_p, kfox_p, vfox_p, logf_p, _, nf_p, qm_p, km_p, vme_p, vmo_p, qf_p, kf_p, vfe_p, vfo_p,
     sga_p, sgb_p) = main
    (ckv_m, kr_m, kfox_m, vfox_m, logf_m, _, nf_m, _, km_m, vme_m, vmo_m, _, kf_m, vfe_m, vfo_m, _, _) = meta
    (ckv_s, kr_s, kfox_s, vfox_s, logf_s, fcol_s, _, qm_s, _, _, _, qf_s, _, _, _, sga_s, sgb_s) = samp

    tb = _pick_tile(seq, 256)
    r3 = lambda a: a.reshape(b, seq, a.shape[-1])
    o_mla = _flash(r3(qm_p), r3(km_p), r3(vme_p), r3(vmo_p), km_m, vme_m, vmo_m, None, None, 1, tb)
    nf_meta = nf_m[0] - nf_m[0][:, N_META - 1:N_META]
    nf_blocks = jnp.swapaxes(nf_p.reshape(b, FOX_HEADS, seq // tb, tb), 1, 2)
    o_fox = _flash(r3(qf_p), r3(kf_p), r3(vfe_p), r3(vfo_p), kf_m, vfe_m, vfo_m, nf_blocks, nf_meta, 2, tb)

    pp = _pick_tile(n_pages, 16)
    rows = n_q * MLA_HEADS
    eye_h = jnp.eye(MLA_HEADS, dtype=BF16)
    q4 = qm_s.reshape(bd, n_q, MLA_HEADS, HEAD_SLOT)
    qn_bd = (q4[:, :, :, None, :NOPE_DIM] * eye_h[None, None, :, :, None]).reshape(bd, rows, MLA_HEADS * NOPE_DIM)
    qr = q4[..., NOPE_DIM:QK_DIM].reshape(bd, rows, ROPE_DIM)
    half = ROPE_DIM // 2
    inv_freq = ROPE_THETA ** (-jnp.arange(half, dtype=F32) / half)
    ang = jnp.arange(past, dtype=F32).reshape(n_pages // pp, 1, pp * PAGE_SIZE) * inv_freq[None, :, None]
    ang_new = (past + jnp.minimum(jnp.arange(PAGE_SIZE), n_q - 1)).astype(F32)[None, :] * inv_freq[:, None]
    pad_new = lambda a: jnp.pad(a.reshape(bd, n_q, a.shape[-1]), ((0, 0), (0, PAGE_SIZE - n_q), (0, 0)))
    pad_new_t = lambda a: jnp.pad(jnp.swapaxes(a.reshape(bd, n_q, a.shape[-1]), 1, 2),
                                  ((0, 0), (0, 0), (0, PAGE_SIZE - n_q)))
    w_ukv0 = w_ukv[0]
    wuk_c = w_ukv0[:, :, :NOPE_DIM].reshape(c_kv, MLA_HEADS * NOPE_DIM).astype(BF16)
    o_mla_s = _mla_decode(
        page_table, cache_mla_latent[0], jnp.swapaxes(cache_mla_krope[0], 1, 2), jnp.cos(ang), jnp.sin(ang),
        qn_bd, qr, pad_new(ckv_s), pad_new_t(kr_s), jnp.cos(ang_new), jnp.sin(ang_new),
        jnp.tile(g_mla_k[0][:NOPE_DIM].astype(F32), MLA_HEADS).reshape(1, -1),
        g_mla_k[0][NOPE_DIM:].astype(F32).reshape(ROPE_DIM, 1),
        wuk_c, wuk_c.T, pw["wuv"], pp)

    suf_new = jnp.pad(-jnp.swapaxes(fcol_s.reshape(bd, n_q, FOX_HEADS), 1, 2), ((0, 0), (0, 0), (0, PAGE_SIZE - n_q)))
    qf4 = qf_s.reshape(bd, n_q, FOX_HEADS, 1, HEAD_SLOT)
    pair_of = (jnp.arange(FOX_HEADS)[:, None] // 2 == jnp.arange(FOX_HEADS // 2)[None, :]).astype(BF16)
    qf_bd = (qf4 * pair_of[None, None, :, :, None]).reshape(bd, rows, fw)
    dim_major = lambda c: jnp.transpose(c, (0, 2, 3, 1)).reshape(c.shape[0], fw, PAGE_SIZE)
    o_fox_s = _fox_decode(page_table, dim_major(cache_fox_k[0]), dim_major(cache_fox_v[0]),
                          jnp.swapaxes(cache_fox_logf[0], 1, 2), qf_bd,
                          pad_new_t(kfox_s), pad_new_t(vfox_s), suf_new, pp)

    mw = {
        "wba": w_branch_mla[0].astype(BF16), "wbb": w_branch_fox[0].astype(BF16), "wo": w_out[0].astype(BF16),
        "fn": ffn_norm[0].reshape(1, -1).astype(F32),
        "wr": _pad_cols(w_router[0].astype(F32), LANES), "br": _pad_cols(b_router[0].reshape(1, -1).astype(F32), LANES),
        "n_experts": n_experts,
    }
    n_tok = b * seq + ns
    tmg = _pick_tile(ns, 512)
    assert (b * seq) % tmg == 0
    flat = lambda a: a.reshape(-1, a.shape[-1])
    bufs = _merge(flat(x_prompt), flat(o_mla), flat(o_fox), sga_p, sgb_p, mw, tmg, n_tok, 0, None)
    h2, xn2, route, cnt = _merge(flat(x_sample), flat(o_mla_s), flat(o_fox_s), sga_s, sgb_s, mw, tmg, n_tok,
                                 b * seq // tmg, bufs)

    g = _moe_rows(xn2, route, cnt, w_gate[0], b_gate[0], w_up[0], b_up[0], w_down[0], b_down[0], tmg,
                  _pick_tile(n_tok * TOP_K, 512))
    y_prompt = _combine(g, route, h2, tmg, 0, b * seq // tmg).reshape(b, seq, d)
    y_sample = _combine(g, route, h2, tmg, b * seq // tmg, ns // tmg).reshape(bd, n_q, d)

    def state_p(main_a, meta_a, tail):
        m = jnp.broadcast_to(meta_a[:N_META].reshape((1, N_META) + tail), (b, N_META) + tail)
        return jnp.concatenate([m, main_a.reshape((b, seq) + tail)], axis=1)[None]

    def state_s(a, tail):
        return a.reshape((1, bd, n_q) + tail)

    return (y_prompt, y_sample,
            state_p(ckv_p, ckv_m, (c_kv,)), state_p(kr_p, kr_m, (ROPE_DIM,)),
            state_p(kfox_p, kfox_m, (FOX_HEADS, FOX_DIM)), state_p(vfox_p, vfox_m, (FOX_HEADS, FOX_DIM)),
            state_p(logf_p, logf_m, (FOX_HEADS,)),
            state_s(ckv_s, (c_kv,)), state_s(kr_s, (ROPE_DIM,)),
            state_s(kfox_s, (FOX_HEADS, FOX_DIM)), state_s(vfox_s, (FOX_HEADS, FOX_DIM)),
            state_s(logf_s, (FOX_HEADS,)))
```

```python
import functools

import jax
import jax.numpy as jnp
import numpy as np
from jax import lax
from jax.experimental import pallas as pl
from jax.experimental.pallas import tpu as pltpu

N_META = 16
MLA_HEADS = 8
NOPE_DIM = 64
ROPE_DIM = 32
V_DIM = 64
QK_DIM = NOPE_DIM + ROPE_DIM
ROPE_THETA = 10000.0
LOG2E = 1.4426950408889634
MLA_SCALE = QK_DIM ** -0.5 * LOG2E
FOX_HEADS = 8
FOX_DIM = 64
FOX_SCALE = FOX_DIM ** -0.5 * LOG2E
TOP_K = 4
SWIGLU_ALPHA = 1.702
SWIGLU_LIMIT = 7.0
NORM_EPS = 1e-6
PAGE_SIZE = 128

LANES = 128
VMEM_LIMIT = 56 * 1024 * 1024
HEAD_SLOT = LANES
NEG_INF = float("-inf")

F32 = jnp.float32
BF16 = jnp.bfloat16


def _dot(a, b):
    return jnp.dot(a, b, preferred_element_type=F32)


def _dot_nt(a, b):
    return lax.dot_general(a, b, (((1,), (1,)), ((), ())), preferred_element_type=F32)


def _const_spec(shape):
    n = len(shape)
    return pl.BlockSpec(shape, lambda *_: (0,) * n)


def _rope_lanes(x, c, s1, s2):
    return x * c + pltpu.roll(x, LANES - ROPE_DIM // 2, 1) * s1 + pltpu.roll(x, ROPE_DIM // 2, 1) * s2


def _proj_body(x_ref, c_ref, s1_ref, s2_ref, tri_ref, an_ref, wq_ref, gql_ref, wuq_ref, gq_ref,
               wkv_ref, gkv_ref, wkr0_ref, wkr64_ref, wuk_ref, gk_ref, wuv_ref,
               wfq_ref, gfq_ref, wfk_ref, gfk_ref, wfv_ref, wf3_ref, bf3_ref, wga_ref, wgb_ref,
               ckv_o, kr_o, kfox_o, vfox_o, logf_o, fcol_o, nf_o,
               qm_o, km_o, vme_o, vmo_o, qf_o, kf_o, vfe_o, vfo_o, sga_o, sgb_o,
               carry_scr):
    tm = x_ref.shape[0]
    x = x_ref[...]
    xn = (x * lax.rsqrt(jnp.mean(x * x, axis=-1, keepdims=True) + NORM_EPS) * an_ref[...]).astype(BF16)
    c, s1, s2 = c_ref[...], s1_ref[...], s2_ref[...]
    lane = lax.broadcasted_iota(jnp.int32, (tm, LANES), 1)
    low = lane < FOX_DIM

    ql = _dot(xn, wq_ref[...])
    qn = (ql * lax.rsqrt(jnp.mean(ql * ql, axis=-1, keepdims=True) + NORM_EPS) * gql_ref[...]).astype(BF16)
    q = _dot(qn, wuq_ref[...])
    gq = gq_ref[...]
    for h in range(MLA_HEADS):
        qh = q[:, h * HEAD_SLOT:(h + 1) * HEAD_SLOT]
        r = lax.rsqrt(jnp.sum(qh * qh, axis=-1, keepdims=True) * (1.0 / QK_DIM) + NORM_EPS)
        qm_o[:, h * HEAD_SLOT:(h + 1) * HEAD_SLOT] = _rope_lanes(qh * r * gq, c, s1, s2).astype(BF16)

    kvl = _dot(xn, wkv_ref[...])
    ckv = kvl * lax.rsqrt(jnp.mean(kvl * kvl, axis=-1, keepdims=True) + NORM_EPS) * gkv_ref[...]
    ckv_o[...] = ckv
    c16 = ckv.astype(BF16)
    kr_o[...] = _dot(xn, wkr0_ref[...])[:, :ROPE_DIM]
    kr = _dot(xn, wkr64_ref[...])
    ssq_r = jnp.sum(kr * kr, axis=-1, keepdims=True)
    gk = gk_ref[...]
    kr_rot = _rope_lanes(kr * gk, c, s1, s2)
    kn = _dot(c16, wuk_ref[...])
    for h in range(MLA_HEADS):
        kh = kn[:, h * HEAD_SLOT:(h + 1) * HEAD_SLOT]
        r = lax.rsqrt((jnp.sum(kh * kh, axis=-1, keepdims=True) + ssq_r) * (1.0 / QK_DIM) + NORM_EPS)
        km_o[:, h * HEAD_SLOT:(h + 1) * HEAD_SLOT] = ((kh * gk + kr_rot) * r).astype(BF16)
    vm = _dot(c16, wuv_ref[...])
    for p in range(MLA_HEADS // 2):
        vp = vm[:, p * LANES:(p + 1) * LANES]
        vme_o[:, p * LANES:(p + 1) * LANES] = jnp.where(low, vp, 0.0).astype(BF16)
        vmo_o[:, p * LANES:(p + 1) * LANES] = jnp.where(low, 0.0, vp).astype(BF16)

    def pair_norm(t):
        sq = t * t
        lo = jnp.sum(jnp.where(low, sq, 0.0), axis=-1, keepdims=True)
        hi = jnp.sum(jnp.where(low, 0.0, sq), axis=-1, keepdims=True)
        r_lo = lax.rsqrt(lo * (1.0 / FOX_DIM) + NORM_EPS)
        r_hi = lax.rsqrt(hi * (1.0 / FOX_DIM) + NORM_EPS)
        return t * jnp.where(low, r_lo, r_hi)

    fq = _dot(xn, wfq_ref[...])
    fk = _dot(xn, wfk_ref[...])
    fv = _dot(xn, wfv_ref[...])
    vfox_o[...] = fv
    for p in range(FOX_HEADS // 2):
        sl = slice(p * LANES, (p + 1) * LANES)
        qp = pair_norm(fq[:, sl]) * gfq_ref[:, sl]
        qf_o[:, (2 * p) * HEAD_SLOT:(2 * p + 1) * HEAD_SLOT] = jnp.where(low, qp, 0.0).astype(BF16)
        qf_o[:, (2 * p + 1) * HEAD_SLOT:(2 * p + 2) * HEAD_SLOT] = jnp.where(low, 0.0, qp).astype(BF16)
        kp = pair_norm(fk[:, sl]) * gfk_ref[:, sl]
        kfox_o[:, sl] = kp
        kf_o[:, sl] = kp.astype(BF16)
        vp = fv[:, sl]
        vfe_o[:, sl] = jnp.where(low, vp, 0.0).astype(BF16)
        vfo_o[:, sl] = jnp.where(low, 0.0, vp).astype(BF16)

    fl = _dot(xn, wf3_ref[...]) + bf3_ref[...]
    lf = jnp.minimum(fl, 0.0) - jnp.log(1.0 + jnp.exp(-jnp.abs(fl)))
    logf_o[...] = lf[:, :FOX_HEADS]
    hi = lf.astype(BF16)
    r1 = lf - hi.astype(F32)
    mid = r1.astype(BF16)
    lo = (r1 - mid.astype(F32)).astype(BF16)
    piece = jnp.where(lane < FOX_HEADS, hi.astype(F32), jnp.where(lane < 2 * FOX_HEADS, mid.astype(F32), lo.astype(F32)))
    piece = jnp.where(lane < 3 * FOX_HEADS, piece, 0.0).astype(BF16)
    f3 = _dot(tri_ref[...], piece)
    f = f3 + pltpu.roll(f3, LANES - FOX_HEADS, 1) + pltpu.roll(f3, LANES - 2 * FOX_HEADS, 1)

    @pl.when(pl.program_id(1) == 0)
    def _():
        carry_scr[...] = jnp.zeros_like(carry_scr)

    f = f + carry_scr[0:1, :]
    carry_scr[0:1, :] = f[tm - 1:tm, :]
    fcol_o[...] = f[:, :FOX_HEADS]
    nf_o[0] = jnp.transpose(f * -LOG2E)[:FOX_HEADS, :]

    sga_o[...] = jax.nn.sigmoid(_dot(xn, wga_ref[...])).astype(BF16)
    sgb_o[...] = jax.nn.sigmoid(_dot(xn, wgb_ref[...])).astype(BF16)


def _proj(x, tables, tri, pw, nb, nt, tm):
    rows, d = x.shape
    c_kv = pw["wkv"].shape[1]
    fw = pw["wfq"].shape[1]
    slots = MLA_HEADS * HEAD_SLOT

    def row_spec(width):
        return pl.BlockSpec((tm, width), lambda b, i: (b * nt + i, 0))

    tab_spec = pl.BlockSpec((tm, LANES), lambda b, i: (i, 0))
    weights = [pw[k] for k in ("an", "wq", "gql", "wuq", "gq", "wkv", "gkv", "wkr0", "wkr64", "wuk", "gk",
                               "wuv", "wfq", "gfq", "wfk", "gfk", "wfv", "wf3", "bf3", "wga", "wgb")]
    in_specs = [row_spec(d), tab_spec, tab_spec, tab_spec, _const_spec(tri.shape)]
    in_specs += [_const_spec(w.shape) for w in weights]
    out_shapes = [
        jax.ShapeDtypeStruct((rows, c_kv), F32),
        jax.ShapeDtypeStruct((rows, ROPE_DIM), F32),
        jax.ShapeDtypeStruct((rows, fw), F32),
        jax.ShapeDtypeStruct((rows, fw), F32),
        jax.ShapeDtypeStruct((rows, FOX_HEADS), F32),
        jax.ShapeDtypeStruct((rows, FOX_HEADS), F32),
        jax.ShapeDtypeStruct((nb, FOX_HEADS, nt * tm), F32),
        jax.ShapeDtypeStruct((rows, slots), BF16),
        jax.ShapeDtypeStruct((rows, slots), BF16),
        jax.ShapeDtypeStruct((rows, fw), BF16),
        jax.ShapeDtypeStruct((rows, fw), BF16),
        jax.ShapeDtypeStruct((rows, slots), BF16),
        jax.ShapeDtypeStruct((rows, fw), BF16),
        jax.ShapeDtypeStruct((rows, fw), BF16),
        jax.ShapeDtypeStruct((rows, fw), BF16),
        jax.ShapeDtypeStruct((rows, d), BF16),
        jax.ShapeDtypeStruct((rows, d), BF16),
    ]
    out_specs = [row_spec(s.shape[1]) for s in out_shapes]
    out_specs[6] = pl.BlockSpec((1, FOX_HEADS, tm), lambda b, i: (b, 0, i))
    return pl.pallas_call(
        _proj_body,
        grid=(nb, nt),
        in_specs=in_specs,
        out_specs=out_specs,
        out_shape=out_shapes,
        scratch_shapes=[pltpu.VMEM((8, LANES), F32)],
        compiler_params=pltpu.CompilerParams(dimension_semantics=("arbitrary", "arbitrary"),
                                             vmem_limit_bytes=VMEM_LIMIT),
        name="proj",
    )(x, *tables, tri, *weights)


def _flash_body(q_ref, k_ref, ve_ref, vo_ref, km_ref, vem_ref, vom_ref, *rest, k_per_slot, has_bias, tb):
    if has_bias:
        nf_ref, nfm_ref, o_ref, m_scr, l_scr, acc_scr = rest
    else:
        o_ref, m_scr, l_scr, acc_scr = rest
    qi = pl.program_id(1)
    n_meta = km_ref.shape[0]

    def step(keys, values, bias, mask):
        scores = []
        for h in range(MLA_HEADS):
            s = _dot_nt(q_ref[0, :, h * HEAD_SLOT:(h + 1) * HEAD_SLOT], keys(h))
            if bias is not None:
                s = s + bias(h)
            scores.append(s if mask is None else jnp.where(mask, s, NEG_INF))
        probs = []
        for h in range(MLA_HEADS):
            s = scores[h]
            m_old = m_scr[h]
            m_new = jnp.maximum(m_old, jnp.max(s, axis=-1, keepdims=True))
            alpha = jnp.exp2(m_old - m_new)
            tiles = [jnp.exp2(s[:, c * LANES:(c + 1) * LANES] - m_new) for c in range(s.shape[1] // LANES)]
            l_scr[h] = l_scr[h] * alpha + functools.reduce(lambda a, b_: a + b_, tiles)
            m_scr[h] = m_new
            acc_scr[h] = acc_scr[h] * alpha
            probs.append(jnp.concatenate(tiles, axis=1).astype(BF16) if len(tiles) > 1 else tiles[0].astype(BF16))
        for h in range(MLA_HEADS):
            acc_scr[h] = acc_scr[h] + _dot(probs[h], values(h))

    def slots(h):
        return (h // k_per_slot) * LANES, (h // 2) * LANES, h % 2 == 0

    m_scr[...] = jnp.full_like(m_scr, NEG_INF)
    l_scr[...] = jnp.zeros_like(l_scr)
    acc_scr[...] = jnp.zeros_like(acc_scr)
    step(lambda h: km_ref[:, slots(h)[0]:slots(h)[0] + LANES],
         lambda h: (vem_ref if slots(h)[2] else vom_ref)[:, slots(h)[1]:slots(h)[1] + LANES],
         (lambda h: nfm_ref[h:h + 1, :]) if has_bias else None,
         lax.broadcasted_iota(jnp.int32, (tb, n_meta), 1) < N_META)

    def block(j, mask):
        start = pl.multiple_of(j * tb, tb)
        step(lambda h: k_ref[0, pl.ds(start, tb), slots(h)[0]:slots(h)[0] + LANES],
             lambda h: (ve_ref if slots(h)[2] else vo_ref)[0, pl.ds(start, tb), slots(h)[1]:slots(h)[1] + LANES],
             (lambda h: nf_ref[0, j, h:h + 1, :]) if has_bias else None, mask)

    def full_block(j, carry):
        block(j, None)
        return carry

    lax.fori_loop(0, qi, full_block, 0)
    block(qi, lax.broadcasted_iota(jnp.int32, (tb, tb), 1) <= lax.broadcasted_iota(jnp.int32, (tb, tb), 0))
    for p in range(MLA_HEADS // 2):
        inv = [1.0 / jnp.sum(l_scr[h], axis=-1, keepdims=True) for h in (2 * p, 2 * p + 1)]
        out = acc_scr[2 * p] * inv[0] + acc_scr[2 * p + 1] * inv[1]
        o_ref[0, :, p * LANES:(p + 1) * LANES] = out.astype(o_ref.dtype)


def _flash(q, k, ve, vo, km, vem, vom, nf, nfm, k_per_slot, tb):
    b, t, _ = q.shape
    has_bias = nf is not None
    nq = t // tb
    in_specs = [
        pl.BlockSpec((1, tb, q.shape[2]), lambda bi, i: (bi, i, 0)),
        pl.BlockSpec((1, t, k.shape[2]), lambda bi, i: (bi, 0, 0)),
        pl.BlockSpec((1, t, ve.shape[2]), lambda bi, i: (bi, 0, 0)),
        pl.BlockSpec((1, t, vo.shape[2]), lambda bi, i: (bi, 0, 0)),
        _const_spec(km.shape), _const_spec(vem.shape), _const_spec(vom.shape),
    ]
    args = [q, k, ve, vo, km, vem, vom]
    if has_bias:
        in_specs += [pl.BlockSpec((1, nq, FOX_HEADS, tb), lambda bi, i: (bi, 0, 0, 0)), _const_spec(nfm.shape)]
        args += [nf, nfm]
    return pl.pallas_call(
        functools.partial(_flash_body, k_per_slot=k_per_slot, has_bias=has_bias, tb=tb),
        grid=(b, nq),
        in_specs=in_specs,
        out_specs=pl.BlockSpec((1, tb, ve.shape[2]), lambda bi, i: (bi, i, 0)),
        out_shape=jax.ShapeDtypeStruct((b, t, ve.shape[2]), BF16),
        scratch_shapes=[pltpu.VMEM((MLA_HEADS, tb, LANES), F32)] * 3,
        compiler_params=pltpu.CompilerParams(dimension_semantics=("arbitrary", "arbitrary"),
                                             vmem_limit_bytes=VMEM_LIMIT),
        name="flash_fox" if has_bias else "flash_mla",
    )(*args)


def _softmax_step(s, m_scr, l_scr):
    m_old = m_scr[...]
    m_new = jnp.maximum(m_old, jnp.max(s, axis=-1, keepdims=True))
    alpha = jnp.exp2(m_old - m_new)
    p = jnp.exp2(s - m_new)
    l_scr[...] = l_scr[...] * alpha + jnp.sum(p, axis=-1, keepdims=True)
    m_scr[...] = m_new
    return p, alpha


def _new_token_mask(rows, n_new):
    t = lax.broadcasted_iota(jnp.int32, (rows, PAGE_SIZE), 1)
    qidx = lax.broadcasted_iota(jnp.int32, (rows, PAGE_SIZE), 0) // MLA_HEADS
    return (t <= qidx) & (t < n_new)


def _fold_heads(full, n_q, width):
    rows = full.shape[0]
    lane_head = lax.broadcasted_iota(jnp.int32, (rows, full.shape[1]), 1) // width
    row_head = lax.broadcasted_iota(jnp.int32, (rows, full.shape[1]), 0) % MLA_HEADS
    kept = jnp.where(lane_head == row_head, full, 0.0)
    return jnp.sum(kept.reshape(n_q, MLA_HEADS, full.shape[1]), axis=1)


def _mla_dec_body(pt_ref, *refs, pp, n_q):
    lat_refs = refs[:pp]
    kr_refs = refs[pp:2 * pp]
    (cos_ref, sin_ref, qn_ref, qr_ref, latn_ref, krn_ref, cosn_ref, sinn_ref,
     gkn_ref, gkr_ref, wuk_ref, wukt_ref, wuv_ref, o_ref, m_scr, l_scr, acc_scr, lhs_scr) = refs[2 * pp:]
    g = pl.program_id(1)
    rows = n_q * MLA_HEADS
    half = ROPE_DIM // 2
    n_up = wukt_ref.shape[0]

    @pl.when(g == 0)
    def _():
        m_scr[...] = jnp.full_like(m_scr, NEG_INF)
        l_scr[...] = jnp.zeros_like(l_scr)
        acc_scr[...] = jnp.zeros_like(acc_scr)
        lhs_scr[:n_up, :] = wukt_ref[...]
        qn = (qn_ref[0].astype(F32) * gkn_ref[...]).astype(BF16)
        lhs_scr[n_up:, :] = _dot_nt(qn, wuk_ref[...]).astype(BF16)

    def chunk(c16, krt, cos_t, sin_t, mask):
        t = c16.shape[0]
        big = _dot_nt(lhs_scr[...], c16)
        knt = big[:n_up]
        ssq_n = jnp.sum((knt * knt).reshape(MLA_HEADS, NOPE_DIM, t), axis=1)
        ssq_r = jnp.sum(krt * krt, axis=0, keepdims=True)
        r = lax.rsqrt((ssq_n + ssq_r) * (1.0 / QK_DIM) + NORM_EPS)
        kg = krt * gkr_ref[...]
        x1, x2 = kg[:half], kg[half:]
        k_rot = jnp.concatenate([x1 * cos_t - x2 * sin_t, x2 * cos_t + x1 * sin_t], axis=0).astype(BF16)
        s = (big[n_up:] + _dot(qr_ref[0], k_rot)) * jnp.concatenate([r] * n_q, axis=0)
        if mask is not None:
            s = jnp.where(mask, s, NEG_INF)
        p, alpha = _softmax_step(s, m_scr, l_scr)
        acc_scr[...] = acc_scr[...] * alpha + _dot(p.astype(BF16), c16)

    chunk(jnp.concatenate([lat_refs[i][0].astype(BF16) for i in range(pp)], axis=0),
          jnp.concatenate([kr_refs[i][0] for i in range(pp)], axis=1), cos_ref[0], sin_ref[0], None)

    @pl.when(g == pl.num_programs(1) - 1)
    def _():
        chunk(latn_ref[0].astype(BF16), krn_ref[0], cosn_ref[...], sinn_ref[...], _new_token_mask(rows, n_q))
        o_lat = (acc_scr[...] * (1.0 / l_scr[...])).astype(BF16)
        o_ref[0] = _fold_heads(_dot(o_lat, wuv_ref[...]), n_q, V_DIM).astype(o_ref.dtype)


def _mla_decode(page_table, cache_lat, cache_kr, cos_t, sin_t, qn_bd, qr, lat_new, kr_new, cos_new, sin_new,
                gkn, gkr, wuk, wukt, wuv, pp):
    bd, n_pages = page_table.shape
    n_q = qr.shape[1] // MLA_HEADS
    rows = qr.shape[1]
    c_kv = cache_lat.shape[-1]

    def page_spec(shape, i):
        return pl.BlockSpec((1,) + shape, lambda b, g, pt: (pt[b, g * pp + i], 0, 0))

    def batch_spec(shape):
        return pl.BlockSpec((1,) + shape, lambda b, g, pt: (b, 0, 0))

    def const_spec(shape):
        n = len(shape)
        return pl.BlockSpec(shape, lambda b, g, pt: (0,) * n)

    in_specs = [page_spec((PAGE_SIZE, c_kv), i) for i in range(pp)]
    in_specs += [page_spec((ROPE_DIM, PAGE_SIZE), i) for i in range(pp)]
    in_specs += [
        pl.BlockSpec((1, ROPE_DIM // 2, pp * PAGE_SIZE), lambda b, g, pt: (g, 0, 0)),
        pl.BlockSpec((1, ROPE_DIM // 2, pp * PAGE_SIZE), lambda b, g, pt: (g, 0, 0)),
        batch_spec(qn_bd.shape[1:]), batch_spec(qr.shape[1:]),
        batch_spec(lat_new.shape[1:]), batch_spec(kr_new.shape[1:]),
        const_spec(cos_new.shape), const_spec(sin_new.shape),
        const_spec(gkn.shape), const_spec(gkr.shape), const_spec(wuk.shape), const_spec(wukt.shape),
        const_spec(wuv.shape),
    ]
    grid_spec = pltpu.PrefetchScalarGridSpec(
        num_scalar_prefetch=1,
        grid=(bd, n_pages // pp),
        in_specs=in_specs,
        out_specs=pl.BlockSpec((1, n_q, wuv.shape[1]), lambda b, g, pt: (b, 0, 0)),
        scratch_shapes=[pltpu.VMEM((rows, 1), F32), pltpu.VMEM((rows, 1), F32),
                        pltpu.VMEM((rows, c_kv), F32), pltpu.VMEM((wukt.shape[0] + rows, c_kv), BF16)],
    )
    return pl.pallas_call(
        functools.partial(_mla_dec_body, pp=pp, n_q=n_q),
        grid_spec=grid_spec,
        out_shape=jax.ShapeDtypeStruct((bd, n_q, wuv.shape[1]), BF16),
        compiler_params=pltpu.CompilerParams(dimension_semantics=("arbitrary", "arbitrary"),
                                             vmem_limit_bytes=VMEM_LIMIT),
        name="mla_decode",
    )(page_table, *([cache_lat] * pp), *([cache_kr] * pp), cos_t, sin_t, qn_bd, qr, lat_new, kr_new,
      cos_new, sin_new, gkn, gkr, wuk, wukt, wuv)


def _fox_dec_body(pt_ref, *refs, pp, n_q):
    k_refs = refs[:pp]
    v_refs = refs[pp:2 * pp]
    lf_refs = refs[2 * pp:3 * pp]
    us_ref, q_ref, kn_ref, vn_ref, sufn_ref, o_ref, m_scr, l_scr, acc_scr, suf_scr = refs[3 * pp:]
    g = pl.program_id(1)
    rows = n_q * FOX_HEADS

    def chunk(kt16, vt16, bias, mask):
        s = _dot(q_ref[0], kt16) + jnp.concatenate([bias] * n_q, axis=0)
        if mask is not None:
            s = jnp.where(mask, s, NEG_INF)
        p, alpha = _softmax_step(s, m_scr, l_scr)
        acc_scr[...] = acc_scr[...] * alpha + _dot_nt(p.astype(BF16), vt16)

    @pl.when(g == 0)
    def _():
        m_scr[...] = jnp.full_like(m_scr, NEG_INF)
        l_scr[...] = jnp.zeros_like(l_scr)
        acc_scr[...] = jnp.zeros_like(acc_scr)
        suf_scr[...] = jnp.zeros_like(suf_scr)
        chunk(kn_ref[0].astype(BF16), vn_ref[0].astype(BF16), sufn_ref[0], _new_token_mask(rows, n_q))

    lfs = [lf_refs[i][0] for i in range(pp)]
    pieces = []
    for lf in lfs:
        hi = lf.astype(BF16).astype(F32)
        mid = (lf - hi).astype(BF16).astype(F32)
        pieces += [hi, mid, lf - hi - mid]
    w = _dot(jnp.concatenate(pieces, axis=0).astype(BF16), us_ref[...])
    carry = suf_scr[...]
    biases = [None] * pp
    for i in reversed(range(pp)):
        base = 3 * FOX_HEADS * i
        within = (w[base:base + FOX_HEADS] + w[base + FOX_HEADS:base + 2 * FOX_HEADS]
                  + w[base + 2 * FOX_HEADS:base + 3 * FOX_HEADS])
        biases[i] = (within + carry) * LOG2E
        carry = carry + jnp.sum(lfs[i], axis=1, keepdims=True)
    suf_scr[...] = carry
    chunk(jnp.concatenate([k_refs[i][0].astype(BF16) for i in range(pp)], axis=1),
          jnp.concatenate([v_refs[i][0].astype(BF16) for i in range(pp)], axis=1),
          jnp.concatenate(biases, axis=1), None)

    @pl.when(g == pl.num_programs(1) - 1)
    def _():
        out = acc_scr[...] * (1.0 / l_scr[...])
        o_ref[0] = _fold_heads(out, n_q, FOX_DIM).astype(o_ref.dtype)


def _fox_decode(page_table, cache_kt, cache_vt, cache_lf, q_bd, kt_new, vt_new, suf_new, pp):
    bd, n_pages = page_table.shape
    rows = q_bd.shape[1]
    n_q = rows // FOX_HEADS
    fw = cache_kt.shape[1]
    ng = n_pages // pp
    i_tok = np.arange(PAGE_SIZE)
    u_strict = jnp.asarray(i_tok[:, None] > i_tok[None, :], dtype=BF16)

    def page_spec(shape, i):
        return pl.BlockSpec((1,) + shape, lambda b, g, pt: (pt[b, (ng - 1 - g) * pp + i], 0, 0))

    def batch_spec(shape):
        return pl.BlockSpec((1,) + shape, lambda b, g, pt: (b,) + (0,) * len(shape))

    in_specs = [page_spec((fw, PAGE_SIZE), i) for i in range(pp)]
    in_specs += [page_spec((fw, PAGE_SIZE), i) for i in range(pp)]
    in_specs += [page_spec((FOX_HEADS, PAGE_SIZE), i) for i in range(pp)]
    in_specs += [
        pl.BlockSpec(u_strict.shape, lambda b, g, pt: (0, 0)),
        batch_spec(q_bd.shape[1:]), batch_spec(kt_new.shape[1:]), batch_spec(vt_new.shape[1:]),
        batch_spec(suf_new.shape[1:]),
    ]
    grid_spec = pltpu.PrefetchScalarGridSpec(
        num_scalar_prefetch=1,
        grid=(bd, ng),
        in_specs=in_specs,
        out_specs=pl.BlockSpec((1, n_q, fw), lambda b, g, pt: (b, 0, 0)),
        scratch_shapes=[pltpu.VMEM((rows, 1), F32), pltpu.VMEM((rows, 1), F32), pltpu.VMEM((rows, fw), F32),
                        pltpu.VMEM((FOX_HEADS, PAGE_SIZE), F32)],
    )
    return pl.pallas_call(
        functools.partial(_fox_dec_body, pp=pp, n_q=n_q),
        grid_spec=grid_spec,
        out_shape=jax.ShapeDtypeStruct((bd, n_q, fw), BF16),
        compiler_params=pltpu.CompilerParams(dimension_semantics=("arbitrary", "arbitrary"),
                                             vmem_limit_bytes=VMEM_LIMIT),
        name="fox_decode",
    )(page_table, *([cache_kt] * pp), *([cache_vt] * pp), *([cache_lf] * pp), u_strict, q_bd, kt_new, vt_new,
      suf_new)


ROUTE_IDX, ROUTE_RANK, ROUTE_GATE = 0, TOP_K, 2 * TOP_K


def _lane_pick(x, lane, which):
    return jnp.sum(jnp.where(lane == which, x, 0.0), axis=-1, keepdims=True)


def _merge_body(*refs, n_experts, n_aliased):
    (h_ref, oa_ref, ob_ref, sga_ref, sgb_ref, tri_ref, wba_ref, wbb_ref, wo_ref, fn_ref, wr_ref, br_ref,
     h2_o, xn_o, route_o, cnt_o) = refs[n_aliased:]
    tm = h_ref.shape[0]
    ua = _dot(oa_ref[...], wba_ref[...]) * sga_ref[...].astype(F32)
    ub = _dot(ob_ref[...], wbb_ref[...]) * sgb_ref[...].astype(F32)
    h2 = h_ref[...] + _dot((ua + ub).astype(BF16), wo_ref[...])
    h2_o[...] = h2
    xn = h2 * lax.rsqrt(jnp.mean(h2 * h2, axis=-1, keepdims=True) + NORM_EPS) * fn_ref[...]
    xn_o[...] = xn.astype(BF16)
    x_hi = xn.astype(BF16)
    x_lo = (xn - x_hi.astype(F32)).astype(BF16)
    logits = _dot(jnp.concatenate([x_hi, x_lo, x_hi], axis=1), wr_ref[...]) + br_ref[...]
    lane = lax.broadcasted_iota(jnp.int32, (tm, LANES), 1).astype(F32)
    logits = jnp.where(lane < n_experts, logits, NEG_INF)
    route = jnp.zeros((tm, LANES), F32)
    vals = jnp.full((tm, LANES), NEG_INF, F32)
    onehot = jnp.zeros((tm, LANES), F32)
    args = []
    for k in range(TOP_K):
        top = jnp.max(logits, axis=-1, keepdims=True)
        arg = jnp.min(jnp.where(logits == top, lane, float(LANES)), axis=-1, keepdims=True)
        route = jnp.where(lane == ROUTE_IDX + k, arg, route)
        vals = jnp.where(lane == k, top, vals)
        onehot = onehot + jnp.where(lane == arg, 1.0, 0.0)
        logits = jnp.where(lane == arg, NEG_INF, logits)
        args.append(arg)
    e = jnp.exp(vals - jnp.max(vals, axis=-1, keepdims=True))
    gates = e / jnp.sum(e, axis=-1, keepdims=True)
    earlier = _dot(tri_ref[...], onehot.astype(BF16))
    for k in range(TOP_K):
        route = jnp.where(lane == ROUTE_RANK + k, _lane_pick(earlier, lane, args[k]), route)
    in_gate = (lane >= ROUTE_GATE) & (lane < ROUTE_GATE + TOP_K)
    route_o[...] = jnp.where(in_gate, pltpu.roll(gates, ROUTE_GATE, 1), route)
    cnt_o[0] = jnp.broadcast_to(jnp.sum(onehot, axis=0, keepdims=True), cnt_o.shape[1:])


def _merge(h, oa, ob, sga, sgb, mw, tm, in_offset, n_tiles, n_rows_total, tile_offset, prev):
    d = h.shape[1]
    n_experts = mw["n_experts"]
    n_tiles_total = n_rows_total // tm
    i = np.arange(tm)
    tri_strict = jnp.asarray(i[None, :] < i[:, None], dtype=BF16)

    def row_spec(width):
        return pl.BlockSpec((tm, width), lambda i: (i + in_offset, 0))

    def out_spec(width):
        return pl.BlockSpec((tm, width), lambda i: (i + tile_offset, 0))

    weights = [mw[k] for k in ("wba", "wbb", "wo", "fn", "wr", "br")]
    n_aliased = 0 if prev is None else len(prev)
    aliased = [] if prev is None else list(prev)
    return pl.pallas_call(
        functools.partial(_merge_body, n_experts=n_experts, n_aliased=n_aliased),
        grid=(n_tiles,),
        in_specs=[pl.BlockSpec(memory_space=pl.ANY)] * n_aliased
        + [row_spec(d), row_spec(oa.shape[1]), row_spec(ob.shape[1]), row_spec(d), row_spec(d),
           _const_spec(tri_strict.shape)] + [_const_spec(w.shape) for w in weights],
        out_specs=[out_spec(d), out_spec(d), out_spec(LANES),
                   pl.BlockSpec((1, 8, LANES), lambda i: (i + tile_offset, 0, 0))],
        out_shape=[jax.ShapeDtypeStruct((n_rows_total, d), F32), jax.ShapeDtypeStruct((n_rows_total, d), BF16),
                   jax.ShapeDtypeStruct((n_rows_total, LANES), F32),
                   jax.ShapeDtypeStruct((n_tiles_total, 8, LANES), F32)],
        input_output_aliases={k: k for k in range(n_aliased)},
        compiler_params=pltpu.CompilerParams(dimension_semantics=("arbitrary",), vmem_limit_bytes=VMEM_LIMIT),
        name="merge",
    )(*aliased, h, oa, ob, sga, sgb, tri_strict, *weights)


def _dest_body(route_ref, base_ref, dest_o):
    tm = route_ref.shape[0]
    lane = lax.broadcasted_iota(jnp.int32, (tm, LANES), 1).astype(F32)
    route = route_ref[...]
    base = base_ref[0, 0:1, :]
    out = jnp.zeros((tm, LANES), F32)
    for k in range(TOP_K):
        e_k = _lane_pick(route, lane, float(ROUTE_IDX + k))
        row = _lane_pick(base, lane, e_k) + _lane_pick(route, lane, float(ROUTE_RANK + k))
        out = jnp.where(lane == k, row, out)
    dest_o[...] = out.astype(jnp.int32)


def _dest(route, base, tm):
    rows = route.shape[0]
    return pl.pallas_call(
        _dest_body,
        grid=(rows // tm,),
        in_specs=[pl.BlockSpec((tm, LANES), lambda i: (i, 0)), pl.BlockSpec((1, 8, LANES), lambda i: (i, 0, 0))],
        out_specs=pl.BlockSpec((tm, LANES), lambda i: (i, 0)),
        out_shape=jax.ShapeDtypeStruct((rows, LANES), jnp.int32),
        compiler_params=pltpu.CompilerParams(dimension_semantics=("arbitrary",)),
        name="dest",
    )(route, base)


def _combine_body(*refs, n_aliased):
    refs = refs[n_aliased:]
    g_refs = refs[:TOP_K]
    route_ref, h2_ref, y_o = refs[TOP_K:]
    tm = h2_ref.shape[0]
    lane = lax.broadcasted_iota(jnp.int32, (tm, LANES), 1).astype(F32)
    route = route_ref[...]
    y = h2_ref[...]
    for k in range(TOP_K):
        y = y + _lane_pick(route, lane, float(ROUTE_GATE + k)) * g_refs[k][...].astype(F32)
    y_o[...] = y


def _combine(g, route, h2, tm, tile_offset, n_tiles, out_tiles_total, out_offset, prev):
    n_tok, d = h2.shape
    tiles_total = n_tok // tm
    spec = lambda width: pl.BlockSpec((tm, width), lambda i: (i + tile_offset, 0))
    g_specs = [pl.BlockSpec((tm, d), functools.partial(lambda i, k: (i + tile_offset + k * tiles_total, 0), k=k))
               for k in range(TOP_K)]
    n_aliased = 0 if prev is None else 1
    return pl.pallas_call(
        functools.partial(_combine_body, n_aliased=n_aliased),
        grid=(n_tiles,),
        in_specs=[pl.BlockSpec(memory_space=pl.ANY)] * n_aliased + g_specs + [spec(LANES), spec(d)],
        out_specs=pl.BlockSpec((tm, d), lambda i: (i + out_offset, 0)),
        out_shape=jax.ShapeDtypeStruct((out_tiles_total * tm, d), F32),
        input_output_aliases={0: 0} if prev is not None else {},
        compiler_params=pltpu.CompilerParams(dimension_semantics=("arbitrary",), vmem_limit_bytes=VMEM_LIMIT),
        name="combine",
    )(*([prev] if prev is not None else []), *([g] * TOP_K), route, h2)


def _ffn_body(be_ref, nu_ref, x_ref, wg_ref, bg_ref, wu_ref, bu_ref, wd_ref, bd_ref, o_ref, wg16, wu16, wd16):
    i = pl.program_id(0)

    @pl.when(i < nu_ref[0])
    def _():
        @pl.when((i == 0) | (be_ref[i] != be_ref[jnp.maximum(i - 1, 0)]))
        def _():
            wg16[...] = wg_ref[0].astype(BF16)
            wu16[...] = wu_ref[0].astype(BF16)
            wd16[...] = wd_ref[0].astype(BF16)

        x = x_ref[...]
        g = _dot(x, wg16[...]) + bg_ref[0]
        u = _dot(x, wu16[...]) + bu_ref[0]
        g = jnp.minimum(g, SWIGLU_LIMIT)
        u = jnp.clip(u, -SWIGLU_LIMIT, SWIGLU_LIMIT)
        hid = ((u + 1.0) * g * jax.nn.sigmoid(SWIGLU_ALPHA * g)).astype(BF16)
        o_ref[...] = (_dot(hid, wd16[...]) + bd_ref[0]).astype(o_ref.dtype)


def _ffn(block_e, n_used, x_rows, w_gate, b_gate, w_up, b_up, w_down, b_down, tb):
    n_rows, d = x_rows.shape
    e, _, f = w_gate.shape

    def wspec(shape):
        return pl.BlockSpec((1,) + shape, lambda i, be, nu: (be[i], 0, 0))

    grid_spec = pltpu.PrefetchScalarGridSpec(
        num_scalar_prefetch=2,
        grid=(n_rows // tb,),
        in_specs=[pl.BlockSpec((tb, d), lambda i, be, nu: (i, 0)),
                  wspec((d, f)), wspec((1, f)), wspec((d, f)), wspec((1, f)), wspec((f, d)), wspec((1, d))],
        out_specs=pl.BlockSpec((tb, d), lambda i, be, nu: (i, 0)),
        scratch_shapes=[pltpu.VMEM((d, f), BF16), pltpu.VMEM((d, f), BF16), pltpu.VMEM((f, d), BF16)],
    )
    return pl.pallas_call(
        _ffn_body,
        grid_spec=grid_spec,
        out_shape=jax.ShapeDtypeStruct((n_rows, d), BF16),
        compiler_params=pltpu.CompilerParams(dimension_semantics=("arbitrary",), vmem_limit_bytes=VMEM_LIMIT),
        name="ffn",
    )(block_e, n_used, x_rows, w_gate, b_gate.reshape(e, 1, f), w_up, b_up.reshape(e, 1, f),
      w_down, b_down.reshape(e, 1, d))


def _rope_tables(pos):
    half = ROPE_DIM // 2
    inv_freq = ROPE_THETA ** (-jnp.arange(half, dtype=F32) / half)
    ang = pos.astype(F32)[:, None] * inv_freq[None, :]
    cos, sin = jnp.cos(ang), jnp.sin(ang)
    n = pos.shape[0]
    zeros = jnp.zeros((n, half), F32)
    c = jnp.concatenate([jnp.ones((n, NOPE_DIM), F32), cos, cos, jnp.ones((n, LANES - QK_DIM), F32)], axis=1)
    s1 = jnp.concatenate([jnp.zeros((n, NOPE_DIM), F32), -sin, zeros, jnp.zeros((n, LANES - QK_DIM), F32)], axis=1)
    s2 = jnp.concatenate([jnp.zeros((n, NOPE_DIM), F32), zeros, sin, jnp.zeros((n, LANES - QK_DIM), F32)], axis=1)
    return c, s1, s2


def _pad_cols(w, width, offset=0):
    return jnp.pad(w, ((0, 0), (offset, width - offset - w.shape[1])))


def _proj_weights(attn_norm, w_in, b_forget, g_qlat, w_uq, g_kvlat, w_ukv, g_mla_q, g_mla_k, g_fox_q, g_fox_k):
    q_lora, c_kv = w_uq.shape[0], w_ukv.shape[0]
    fw = FOX_HEADS * FOX_DIM
    d = w_in.shape[0]
    sizes = (q_lora, c_kv, ROPE_DIM, fw, fw, fw, FOX_HEADS, d, d)
    offs = np.concatenate([[0], np.cumsum(sizes)])
    wq, wkv, wkr, wfq, wfk, wfv, wf, wga, wgb = [w_in[:, offs[i]:offs[i + 1]] for i in range(len(sizes))]
    pad_head = LANES - QK_DIM
    wuq = jnp.pad(w_uq, ((0, 0), (0, 0), (0, pad_head))).reshape(q_lora, MLA_HEADS * HEAD_SLOT)
    wuk = jnp.pad(w_ukv[:, :, :NOPE_DIM], ((0, 0), (0, 0), (0, LANES - NOPE_DIM))).reshape(c_kv, MLA_HEADS * HEAD_SLOT)
    wuv = w_ukv[:, :, NOPE_DIM:].reshape(c_kv, MLA_HEADS * V_DIM)
    row = lambda v: v.reshape(1, -1).astype(F32)
    return {
        "an": row(attn_norm), "wq": wq.astype(BF16), "gql": row(g_qlat), "wuq": wuq.astype(BF16),
        "gq": _pad_cols(row(g_mla_q) * MLA_SCALE, LANES),
        "wkv": wkv.astype(BF16), "gkv": row(g_kvlat),
        "wkr0": _pad_cols(wkr, LANES).astype(BF16), "wkr64": _pad_cols(wkr, LANES, NOPE_DIM).astype(BF16),
        "wuk": wuk.astype(BF16), "gk": _pad_cols(row(g_mla_k), LANES), "wuv": wuv.astype(BF16),
        "wfq": wfq.astype(BF16), "gfq": jnp.tile(row(g_fox_q), (1, FOX_HEADS)) * FOX_SCALE,
        "wfk": wfk.astype(BF16), "gfk": jnp.tile(row(g_fox_k), (1, FOX_HEADS)),
        "wfv": wfv.astype(BF16),
        "wf3": _pad_cols(jnp.tile(wf, (1, 3)), LANES).astype(BF16),
        "bf3": _pad_cols(jnp.tile(row(b_forget), (1, 3)), LANES),
        "wga": wga.astype(BF16), "wgb": wgb.astype(BF16),
    }


def _tri(n, group=None):
    i = np.arange(n)
    m = i[None, :] <= i[:, None]
    if group is not None:
        m &= (i[None, :] // group) == (i[:, None] // group)
    return jnp.asarray(m, dtype=BF16)


def _pick_tile(n, target):
    t = min(n, target)
    while n % t:
        t //= 2
    return t


def _moe_rows(xn, route, cnt, w_gate, b_gate, w_up, b_up, w_down, b_down, tm, tb):
    n_tok, d = xn.shape
    n_experts = w_gate.shape[0]
    n_assign = n_tok * TOP_K
    n_tiles = n_tok // tm
    cnt = cnt[:, 0, :n_experts].astype(jnp.int32)
    counts = jnp.sum(cnt, axis=0)
    tile_off = jnp.cumsum(cnt, axis=0) - cnt
    padded = (counts + tb - 1) // tb * tb
    pad_end = jnp.cumsum(padded)
    pad_start = pad_end - padded
    start = jnp.cumsum(counts) - counts
    n_blocks = -(-n_assign // tb) + n_experts
    n_used = (pad_end[-1] // tb).astype(jnp.int32).reshape(1)
    block_start = jnp.arange(n_blocks, dtype=jnp.int32) * tb
    block_e = jnp.minimum(jnp.sum((pad_end[None, :] <= block_start[:, None]).astype(jnp.int32), axis=1),
                          n_experts - 1)
    base = jnp.zeros((n_tiles, 8, LANES), F32).at[:, 0, :n_experts].set((pad_start[None, :] + tile_off).astype(F32))
    dest = _dest(route, base, tm)[:, :TOP_K].T.reshape(-1)
    flat_e = route[:, ROUTE_IDX:ROUTE_IDX + TOP_K].astype(jnp.int32).reshape(-1)
    order = jnp.argsort(flat_e)
    delta = start[block_e] - pad_start[block_e]
    pos = jnp.arange(n_blocks * tb, dtype=jnp.int32).reshape(n_blocks, tb) + delta[:, None]
    src_tok = order[jnp.clip(pos, 0, n_assign - 1)] // TOP_K
    x_rows = xn[src_tok.reshape(-1)]
    y_rows = _ffn(block_e.astype(jnp.int32), n_used, x_rows, w_gate, b_gate, w_up, b_up, w_down, b_down, tb)
    return y_rows[dest]


def kernel(x_prompt, x_sample, cache_mla_latent, cache_mla_krope, cache_fox_k, cache_fox_v, cache_fox_logf,
           page_table, meta_tokens, attn_norm, w_in, b_forget, g_qlat, w_uq, g_kvlat, w_ukv, g_mla_q, g_mla_k,
           g_fox_q, g_fox_k, w_branch_mla, w_branch_fox, w_out, ffn_norm, w_router, b_router, w_gate, b_gate,
           w_up, b_up, w_down, b_down):
    assert attn_norm.shape[0] == 1, "single-layer trunk"
    b, seq, d = x_prompt.shape
    bd, n_q, _ = x_sample.shape
    n_pages = page_table.shape[1]
    n_experts = w_router.shape[-1]
    fw = FOX_HEADS * FOX_DIM
    c_kv = w_ukv.shape[1]

    pw = _proj_weights(attn_norm[0], w_in[0], b_forget[0], g_qlat[0], w_uq[0], g_kvlat[0], w_ukv[0],
                       g_mla_q[0], g_mla_k[0], g_fox_q[0], g_fox_k[0])

    tm = _pick_tile(seq, 512)
    nt = seq // tm
    main = _proj(x_prompt.reshape(b * seq, d), _rope_tables(N_META + jnp.arange(seq)), _tri(tm), pw, b, nt, tm)
    meta_x = jnp.pad(meta_tokens.astype(F32), ((0, LANES - N_META), (0, 0)))
    meta = _proj(meta_x, _rope_tables(jnp.arange(LANES)), _tri(LANES), pw, 1, 1, LANES)
    past = n_pages * PAGE_SIZE
    ns = bd * n_q
    samp = _proj(x_sample.reshape(ns, d), _rope_tables(past + jnp.arange(ns) % n_q), _tri(ns, n_q), pw, 1, 1, ns)

    (ckv_p, kr_p, kfox_p, vfox_p, logf_p, _, nf_p, qm_p, km_p, vme_p, vmo_p, qf_p, kf_p, vfe_p, vfo_p,
     sga_p, sgb_p) = main
    (ckv_m, kr_m, kfox_m, vfox_m, logf_m, _, nf_m, _, km_m, vme_m, vmo_m, _, kf_m, vfe_m, vfo_m, _, _) = meta
    (ckv_s, kr_s, kfox_s, vfox_s, logf_s, fcol_s, _, qm_s, _, _, _, qf_s, _, _, _, sga_s, sgb_s) = samp

    tb = _pick_tile(seq, 256)
    r3 = lambda a: a.reshape(b, seq, a.shape[-1])
    o_mla = _flash(r3(qm_p), r3(km_p), r3(vme_p), r3(vmo_p), km_m, vme_m, vmo_m, None, None, 1, tb)
    nf_meta = nf_m[0] - nf_m[0][:, N_META - 1:N_META]
    nf_blocks = jnp.swapaxes(nf_p.reshape(b, FOX_HEADS, seq // tb, tb), 1, 2)
    o_fox = _flash(r3(qf_p), r3(kf_p), r3(vfe_p), r3(vfo_p), kf_m, vfe_m, vfo_m, nf_blocks, nf_meta, 2, tb)

    pp = _pick_tile(n_pages, 16)
    pp_m = _pick_tile(n_pages, 32)
    rows = n_q * MLA_HEADS
    eye_h = jnp.eye(MLA_HEADS, dtype=BF16)
    q4 = qm_s.reshape(bd, n_q, MLA_HEADS, HEAD_SLOT)
    qn_bd = (q4[:, :, :, None, :NOPE_DIM] * eye_h[None, None, :, :, None]).reshape(bd, rows, MLA_HEADS * NOPE_DIM)
    qr = q4[..., NOPE_DIM:QK_DIM].reshape(bd, rows, ROPE_DIM)
    half = ROPE_DIM // 2
    inv_freq = ROPE_THETA ** (-jnp.arange(half, dtype=F32) / half)
    ang = jnp.arange(past, dtype=F32).reshape(n_pages // pp_m, 1, pp_m * PAGE_SIZE) * inv_freq[None, :, None]
    ang_new = (past + jnp.minimum(jnp.arange(PAGE_SIZE), n_q - 1)).astype(F32)[None, :] * inv_freq[:, None]
    pad_new = lambda a: jnp.pad(a.reshape(bd, n_q, a.shape[-1]), ((0, 0), (0, PAGE_SIZE - n_q), (0, 0)))
    pad_new_t = lambda a: jnp.pad(jnp.swapaxes(a.reshape(bd, n_q, a.shape[-1]), 1, 2),
                                  ((0, 0), (0, 0), (0, PAGE_SIZE - n_q)))
    w_ukv0 = w_ukv[0]
    wuk_c = w_ukv0[:, :, :NOPE_DIM].reshape(c_kv, MLA_HEADS * NOPE_DIM).astype(BF16)
    o_mla_s = _mla_decode(
        page_table, cache_mla_latent[0], jnp.swapaxes(cache_mla_krope[0], 1, 2), jnp.cos(ang), jnp.sin(ang),
        qn_bd, qr, pad_new(ckv_s), pad_new_t(kr_s), jnp.cos(ang_new), jnp.sin(ang_new),
        jnp.tile(g_mla_k[0][:NOPE_DIM].astype(F32), MLA_HEADS).reshape(1, -1),
        g_mla_k[0][NOPE_DIM:].astype(F32).reshape(ROPE_DIM, 1),
        wuk_c, wuk_c.T, pw["wuv"], pp_m)

    suf_new = jnp.pad(-LOG2E * jnp.swapaxes(fcol_s.reshape(bd, n_q, FOX_HEADS), 1, 2),
                      ((0, 0), (0, 0), (0, PAGE_SIZE - n_q)))
    qf4 = qf_s.reshape(bd, n_q, FOX_HEADS, 1, HEAD_SLOT)
    pair_of = (jnp.arange(FOX_HEADS)[:, None] // 2 == jnp.arange(FOX_HEADS // 2)[None, :]).astype(BF16)
    qf_bd = (qf4 * pair_of[None, None, :, :, None]).reshape(bd, rows, fw)
    dim_major = lambda c: jnp.transpose(c, (0, 2, 3, 1)).reshape(c.shape[0], fw, PAGE_SIZE)
    o_fox_s = _fox_decode(page_table, dim_major(cache_fox_k[0]), dim_major(cache_fox_v[0]),
                          jnp.swapaxes(cache_fox_logf[0], 1, 2), qf_bd,
                          pad_new_t(kfox_s), pad_new_t(vfox_s), suf_new, pp)

    wr = _pad_cols(w_router[0].astype(F32), LANES)
    wr_hi = wr.astype(BF16)
    wr_lo = (wr - wr_hi.astype(F32)).astype(BF16)
    mw = {
        "wba": w_branch_mla[0].astype(BF16), "wbb": w_branch_fox[0].astype(BF16), "wo": w_out[0].astype(BF16),
        "fn": ffn_norm[0].reshape(1, -1).astype(F32),
        "wr": jnp.concatenate([wr_hi, wr_hi, wr_lo], axis=0),
        "br": _pad_cols(b_router[0].reshape(1, -1).astype(F32), LANES),
        "n_experts": n_experts,
    }
    tmg = _pick_tile(ns, 512)
    assert (b * seq) % tmg == 0 and ns % tmg == 0
    tiles_p, tiles_s = b * seq // tmg, ns // tmg
    tiles_a = tiles_p // 2
    tiles_b = tiles_p - tiles_a + tiles_s
    flat = lambda a: a.reshape(-1, a.shape[-1])
    prompt_in = (flat(x_prompt), flat(o_mla), flat(o_fox), sga_p, sgb_p)
    sample_in = (flat(x_sample), flat(o_mla_s), flat(o_fox_s), sga_s, sgb_s)
    group_a = _merge(*prompt_in, mw, tmg, 0, tiles_a, tiles_a * tmg, 0, None)
    group_b = _merge(*prompt_in, mw, tmg, tiles_a, tiles_p - tiles_a, tiles_b * tmg, 0, None)
    group_b = _merge(*sample_in, mw, tmg, 0, tiles_s, tiles_b * tmg, tiles_p - tiles_a, group_b)

    def experts(group):
        h2, xn2, route, cnt = group
        g = _moe_rows(xn2, route, cnt, w_gate[0], b_gate[0], w_up[0], b_up[0], w_down[0], b_down[0], tmg,
                      _pick_tile(xn2.shape[0] * TOP_K, 512))
        return g, route, h2

    ga, gb = experts(group_a), experts(group_b)
    y_prompt = _combine(*ga, tmg, 0, tiles_a, tiles_p, 0, None)
    y_prompt = _combine(*gb, tmg, 0, tiles_p - tiles_a, tiles_p, tiles_a, y_prompt).reshape(b, seq, d)
    y_sample = _combine(*gb, tmg, tiles_p - tiles_a, tiles_s, tiles_s, 0, None).reshape(bd, n_q, d)

    def state_p(main_a, meta_a, tail):
        m = jnp.broadcast_to(meta_a[:N_META].reshape((1, N_META) + tail), (b, N_META) + tail)
        return jnp.concatenate([m, main_a.reshape((b, seq) + tail)], axis=1)[None]

    def state_s(a, tail):
        return a.reshape((1, bd, n_q) + tail)

    return (y_prompt, y_sample,
            state_p(ckv_p, ckv_m, (c_kv,)), state_p(kr_p, kr_m, (ROPE_DIM,)),
            state_p(kfox_p, kfox_m, (FOX_HEADS, FOX_DIM)), state_p(vfox_p, vfox_m, (FOX_HEADS, FOX_DIM)),
            state_p(logf_p, logf_m, (FOX_HEADS,)),
            state_s(ckv_s, (c_kv,)), state_s(kr_s, (ROPE_DIM,)),
            state_s(kfox_s, (FOX_HEADS, FOX_DIM)), state_s(vfox_s, (FOX_HEADS, FOX_DIM)),
            state_s(logf_s, (FOX_HEADS,)))
```

```python
import functools

import jax
import jax.numpy as jnp
import numpy as np
from jax import lax
from jax.experimental import pallas as pl
from jax.experimental.pallas import tpu as pltpu

N_META = 16
MLA_HEADS = 8
NOPE_DIM = 64
ROPE_DIM = 32
V_DIM = 64
QK_DIM = NOPE_DIM + ROPE_DIM
ROPE_THETA = 10000.0
LOG2E = 1.4426950408889634
MLA_SCALE = QK_DIM ** -0.5 * LOG2E
FOX_HEADS = 8
FOX_DIM = 64
FOX_SCALE = FOX_DIM ** -0.5 * LOG2E
TOP_K = 4
SWIGLU_ALPHA = 1.702
SWIGLU_LIMIT = 7.0
NORM_EPS = 1e-6
PAGE_SIZE = 128

LANES = 128
VMEM_LIMIT = 56 * 1024 * 1024
HEAD_SLOT = LANES
NEG_INF = float("-inf")

F32 = jnp.float32
BF16 = jnp.bfloat16


def _dot(a, b):
    return jnp.dot(a, b, preferred_element_type=F32)


def _dot_nt(a, b):
    return lax.dot_general(a, b, (((1,), (1,)), ((), ())), preferred_element_type=F32)


def _const_spec(shape):
    n = len(shape)
    return pl.BlockSpec(shape, lambda *_: (0,) * n)


def _rope_lanes(x, c, s1, s2):
    return x * c + pltpu.roll(x, LANES - ROPE_DIM // 2, 1) * s1 + pltpu.roll(x, ROPE_DIM // 2, 1) * s2


def _proj_body(x_ref, c_ref, s1_ref, s2_ref, tri_ref, an_ref, wq_ref, gql_ref, wuq_ref, gq_ref,
               wkv_ref, gkv_ref, wkr0_ref, wkr64_ref, wuk_ref, gk_ref, wuv_ref,
               wfq_ref, gfq_ref, wfk_ref, gfk_ref, wfv_ref, wf3_ref, bf3_ref, wga_ref, wgb_ref,
               ckv_o, kr_o, kfox_o, vfox_o, logf_o, fcol_o, nf_o,
               qm_o, km_o, vme_o, vmo_o, qf_o, kf_o, vfe_o, vfo_o, sga_o, sgb_o,
               carry_scr):
    tm = x_ref.shape[0]
    x = x_ref[...]
    xn = (x * lax.rsqrt(jnp.mean(x * x, axis=-1, keepdims=True) + NORM_EPS) * an_ref[...]).astype(BF16)
    c, s1, s2 = c_ref[...], s1_ref[...], s2_ref[...]
    lane = lax.broadcasted_iota(jnp.int32, (tm, LANES), 1)
    low = lane < FOX_DIM

    ql = _dot(xn, wq_ref[...])
    qn = (ql * lax.rsqrt(jnp.mean(ql * ql, axis=-1, keepdims=True) + NORM_EPS) * gql_ref[...]).astype(BF16)
    q = _dot(qn, wuq_ref[...])
    gq = gq_ref[...]
    for h in range(MLA_HEADS):
        qh = q[:, h * HEAD_SLOT:(h + 1) * HEAD_SLOT]
        r = lax.rsqrt(jnp.sum(qh * qh, axis=-1, keepdims=True) * (1.0 / QK_DIM) + NORM_EPS)
        qm_o[:, h * HEAD_SLOT:(h + 1) * HEAD_SLOT] = _rope_lanes(qh * r * gq, c, s1, s2).astype(BF16)

    kvl = _dot(xn, wkv_ref[...])
    ckv = kvl * lax.rsqrt(jnp.mean(kvl * kvl, axis=-1, keepdims=True) + NORM_EPS) * gkv_ref[...]
    ckv_o[...] = ckv
    c16 = ckv.astype(BF16)
    kr_o[...] = _dot(xn, wkr0_ref[...])[:, :ROPE_DIM]
    kr = _dot(xn, wkr64_ref[...])
    ssq_r = jnp.sum(kr * kr, axis=-1, keepdims=True)
    gk = gk_ref[...]
    kr_rot = _rope_lanes(kr * gk, c, s1, s2)
    kn = _dot(c16, wuk_ref[...])
    for h in range(MLA_HEADS):
        kh = kn[:, h * HEAD_SLOT:(h + 1) * HEAD_SLOT]
        r = lax.rsqrt((jnp.sum(kh * kh, axis=-1, keepdims=True) + ssq_r) * (1.0 / QK_DIM) + NORM_EPS)
        km_o[:, h * HEAD_SLOT:(h + 1) * HEAD_SLOT] = ((kh * gk + kr_rot) * r).astype(BF16)
    vm = _dot(c16, wuv_ref[...])
    for p in range(MLA_HEADS // 2):
        vp = vm[:, p * LANES:(p + 1) * LANES]
        vme_o[:, p * LANES:(p + 1) * LANES] = jnp.where(low, vp, 0.0).astype(BF16)
        vmo_o[:, p * LANES:(p + 1) * LANES] = jnp.where(low, 0.0, vp).astype(BF16)

    def pair_norm(t):
        sq = t * t
        lo = jnp.sum(jnp.where(low, sq, 0.0), axis=-1, keepdims=True)
        hi = jnp.sum(jnp.where(low, 0.0, sq), axis=-1, keepdims=True)
        r_lo = lax.rsqrt(lo * (1.0 / FOX_DIM) + NORM_EPS)
        r_hi = lax.rsqrt(hi * (1.0 / FOX_DIM) + NORM_EPS)
        return t * jnp.where(low, r_lo, r_hi)

    fq = _dot(xn, wfq_ref[...])
    fk = _dot(xn, wfk_ref[...])
    fv = _dot(xn, wfv_ref[...])
    vfox_o[...] = fv
    for p in range(FOX_HEADS // 2):
        sl = slice(p * LANES, (p + 1) * LANES)
        qp = pair_norm(fq[:, sl]) * gfq_ref[:, sl]
        qf_o[:, (2 * p) * HEAD_SLOT:(2 * p + 1) * HEAD_SLOT] = jnp.where(low, qp, 0.0).astype(BF16)
        qf_o[:, (2 * p + 1) * HEAD_SLOT:(2 * p + 2) * HEAD_SLOT] = jnp.where(low, 0.0, qp).astype(BF16)
        kp = pair_norm(fk[:, sl]) * gfk_ref[:, sl]
        kfox_o[:, sl] = kp
        kf_o[:, sl] = kp.astype(BF16)
        vp = fv[:, sl]
        vfe_o[:, sl] = jnp.where(low, vp, 0.0).astype(BF16)
        vfo_o[:, sl] = jnp.where(low, 0.0, vp).astype(BF16)

    fl = _dot(xn, wf3_ref[...]) + bf3_ref[...]
    lf = jnp.minimum(fl, 0.0) - jnp.log(1.0 + jnp.exp(-jnp.abs(fl)))
    logf_o[...] = lf[:, :FOX_HEADS]
    hi = lf.astype(BF16)
    r1 = lf - hi.astype(F32)
    mid = r1.astype(BF16)
    lo = (r1 - mid.astype(F32)).astype(BF16)
    piece = jnp.where(lane < FOX_HEADS, hi.astype(F32), jnp.where(lane < 2 * FOX_HEADS, mid.astype(F32), lo.astype(F32)))
    piece = jnp.where(lane < 3 * FOX_HEADS, piece, 0.0).astype(BF16)
    f3 = _dot(tri_ref[...], piece)
    f = f3 + pltpu.roll(f3, LANES - FOX_HEADS, 1) + pltpu.roll(f3, LANES - 2 * FOX_HEADS, 1)

    @pl.when(pl.program_id(1) == 0)
    def _():
        carry_scr[...] = jnp.zeros_like(carry_scr)

    f = f + carry_scr[0:1, :]
    carry_scr[0:1, :] = f[tm - 1:tm, :]
    fcol_o[...] = f[:, :FOX_HEADS]
    nf_o[0] = jnp.transpose(f * -LOG2E)[:FOX_HEADS, :]

    sga_o[...] = jax.nn.sigmoid(_dot(xn, wga_ref[...])).astype(BF16)
    sgb_o[...] = jax.nn.sigmoid(_dot(xn, wgb_ref[...])).astype(BF16)


def _proj(x, tables, tri, pw, nb, nt, tm):
    rows, d = x.shape
    c_kv = pw["wkv"].shape[1]
    fw = pw["wfq"].shape[1]
    slots = MLA_HEADS * HEAD_SLOT

    def row_spec(width):
        return pl.BlockSpec((tm, width), lambda b, i: (b * nt + i, 0))

    tab_spec = pl.BlockSpec((tm, LANES), lambda b, i: (i, 0))
    weights = [pw[k] for k in ("an", "wq", "gql", "wuq", "gq", "wkv", "gkv", "wkr0", "wkr64", "wuk", "gk",
                               "wuv", "wfq", "gfq", "wfk", "gfk", "wfv", "wf3", "bf3", "wga", "wgb")]
    in_specs = [row_spec(d), tab_spec, tab_spec, tab_spec, _const_spec(tri.shape)]
    in_specs += [_const_spec(w.shape) for w in weights]
    out_shapes = [
        jax.ShapeDtypeStruct((rows, c_kv), F32),
        jax.ShapeDtypeStruct((rows, ROPE_DIM), F32),
        jax.ShapeDtypeStruct((rows, fw), F32),
        jax.ShapeDtypeStruct((rows, fw), F32),
        jax.ShapeDtypeStruct((rows, FOX_HEADS), F32),
        jax.ShapeDtypeStruct((rows, FOX_HEADS), F32),
        jax.ShapeDtypeStruct((nb, FOX_HEADS, nt * tm), F32),
        jax.ShapeDtypeStruct((rows, slots), BF16),
        jax.ShapeDtypeStruct((rows, slots), BF16),
        jax.ShapeDtypeStruct((rows, fw), BF16),
        jax.ShapeDtypeStruct((rows, fw), BF16),
        jax.ShapeDtypeStruct((rows, slots), BF16),
        jax.ShapeDtypeStruct((rows, fw), BF16),
        jax.ShapeDtypeStruct((rows, fw), BF16),
        jax.ShapeDtypeStruct((rows, fw), BF16),
        jax.ShapeDtypeStruct((rows, d), BF16),
        jax.ShapeDtypeStruct((rows, d), BF16),
    ]
    out_specs = [row_spec(s.shape[1]) for s in out_shapes]
    out_specs[6] = pl.BlockSpec((1, FOX_HEADS, tm), lambda b, i: (b, 0, i))
    return pl.pallas_call(
        _proj_body,
        grid=(nb, nt),
        in_specs=in_specs,
        out_specs=out_specs,
        out_shape=out_shapes,
        scratch_shapes=[pltpu.VMEM((8, LANES), F32)],
        compiler_params=pltpu.CompilerParams(dimension_semantics=("arbitrary", "arbitrary"),
                                             vmem_limit_bytes=VMEM_LIMIT),
        name="proj",
    )(x, *tables, tri, *weights)


def _flash_body(q_ref, k_ref, ve_ref, vo_ref, km_ref, vem_ref, vom_ref, *rest, k_per_slot, has_bias, tb):
    if has_bias:
        nf_ref, nfm_ref, o_ref, m_scr, l_scr, acc_scr = rest
    else:
        o_ref, m_scr, l_scr, acc_scr = rest
    qi = pl.program_id(1)
    n_meta = km_ref.shape[0]

    def step(keys, values, bias, mask):
        scores = []
        for h in range(MLA_HEADS):
            s = _dot_nt(q_ref[0, :, h * HEAD_SLOT:(h + 1) * HEAD_SLOT], keys(h))
            if bias is not None:
                s = s + bias(h)
            scores.append(s if mask is None else jnp.where(mask, s, NEG_INF))
        probs = []
        for h in range(MLA_HEADS):
            s = scores[h]
            m_old = m_scr[h]
            m_new = jnp.maximum(m_old, jnp.max(s, axis=-1, keepdims=True))
            alpha = jnp.exp2(m_old - m_new)
            tiles = [jnp.exp2(s[:, c * LANES:(c + 1) * LANES] - m_new) for c in range(s.shape[1] // LANES)]
            l_scr[h] = l_scr[h] * alpha + functools.reduce(lambda a, b_: a + b_, tiles)
            m_scr[h] = m_new
            acc_scr[h] = acc_scr[h] * alpha
            probs.append(jnp.concatenate(tiles, axis=1).astype(BF16) if len(tiles) > 1 else tiles[0].astype(BF16))
        for h in range(MLA_HEADS):
            acc_scr[h] = acc_scr[h] + _dot(probs[h], values(h))

    def slots(h):
        return (h // k_per_slot) * LANES, (h // 2) * LANES, h % 2 == 0

    m_scr[...] = jnp.full_like(m_scr, NEG_INF)
    l_scr[...] = jnp.zeros_like(l_scr)
    acc_scr[...] = jnp.zeros_like(acc_scr)
    step(lambda h: km_ref[:, slots(h)[0]:slots(h)[0] + LANES],
         lambda h: (vem_ref if slots(h)[2] else vom_ref)[:, slots(h)[1]:slots(h)[1] + LANES],
         (lambda h: nfm_ref[h:h + 1, :]) if has_bias else None,
         lax.broadcasted_iota(jnp.int32, (tb, n_meta), 1) < N_META)

    def block(j, mask):
        start = pl.multiple_of(j * tb, tb)
        step(lambda h: k_ref[0, pl.ds(start, tb), slots(h)[0]:slots(h)[0] + LANES],
             lambda h: (ve_ref if slots(h)[2] else vo_ref)[0, pl.ds(start, tb), slots(h)[1]:slots(h)[1] + LANES],
             (lambda h: nf_ref[0, j, h:h + 1, :]) if has_bias else None, mask)

    def full_block(j, carry):
        block(j, None)
        return carry

    lax.fori_loop(0, qi, full_block, 0)
    block(qi, lax.broadcasted_iota(jnp.int32, (tb, tb), 1) <= lax.broadcasted_iota(jnp.int32, (tb, tb), 0))
    for p in range(MLA_HEADS // 2):
        inv = [1.0 / jnp.sum(l_scr[h], axis=-1, keepdims=True) for h in (2 * p, 2 * p + 1)]
        out = acc_scr[2 * p] * inv[0] + acc_scr[2 * p + 1] * inv[1]
        o_ref[0, :, p * LANES:(p + 1) * LANES] = out.astype(o_ref.dtype)


def _flash(q, k, ve, vo, km, vem, vom, nf, nfm, k_per_slot, tb):
    b, t, _ = q.shape
    has_bias = nf is not None
    nq = t // tb
    in_specs = [
        pl.BlockSpec((1, tb, q.shape[2]), lambda bi, i: (bi, i, 0)),
        pl.BlockSpec((1, t, k.shape[2]), lambda bi, i: (bi, 0, 0)),
        pl.BlockSpec((1, t, ve.shape[2]), lambda bi, i: (bi, 0, 0)),
        pl.BlockSpec((1, t, vo.shape[2]), lambda bi, i: (bi, 0, 0)),
        _const_spec(km.shape), _const_spec(vem.shape), _const_spec(vom.shape),
    ]
    args = [q, k, ve, vo, km, vem, vom]
    if has_bias:
        in_specs += [pl.BlockSpec((1, nq, FOX_HEADS, tb), lambda bi, i: (bi, 0, 0, 0)), _const_spec(nfm.shape)]
        args += [nf, nfm]
    return pl.pallas_call(
        functools.partial(_flash_body, k_per_slot=k_per_slot, has_bias=has_bias, tb=tb),
        grid=(b, nq),
        in_specs=in_specs,
        out_specs=pl.BlockSpec((1, tb, ve.shape[2]), lambda bi, i: (bi, i, 0)),
        out_shape=jax.ShapeDtypeStruct((b, t, ve.shape[2]), BF16),
        scratch_shapes=[pltpu.VMEM((MLA_HEADS, tb, LANES), F32)] * 3,
        compiler_params=pltpu.CompilerParams(dimension_semantics=("arbitrary", "arbitrary"),
                                             vmem_limit_bytes=VMEM_LIMIT),
        name="flash_fox" if has_bias else "flash_mla",
    )(*args)


def _softmax_step(s, m_scr, l_scr):
    m_old = m_scr[...]
    m_new = jnp.maximum(m_old, jnp.max(s, axis=-1, keepdims=True))
    alpha = jnp.exp2(m_old - m_new)
    p = jnp.exp2(s - m_new)
    l_scr[...] = l_scr[...] * alpha + jnp.sum(p, axis=-1, keepdims=True)
    m_scr[...] = m_new
    return p, alpha


def _new_token_mask(rows, n_new):
    t = lax.broadcasted_iota(jnp.int32, (rows, PAGE_SIZE), 1)
    qidx = lax.broadcasted_iota(jnp.int32, (rows, PAGE_SIZE), 0) // MLA_HEADS
    return (t <= qidx) & (t < n_new)


def _fold_heads(full, n_q, width):
    rows = full.shape[0]
    lane_head = lax.broadcasted_iota(jnp.int32, (rows, full.shape[1]), 1) // width
    row_head = lax.broadcasted_iota(jnp.int32, (rows, full.shape[1]), 0) % MLA_HEADS
    kept = jnp.where(lane_head == row_head, full, 0.0)
    return jnp.sum(kept.reshape(n_q, MLA_HEADS, full.shape[1]), axis=1)


def _mla_dec_body(pt_ref, *refs, pp, n_q):
    lat_refs = refs[:pp]
    kr_refs = refs[pp:2 * pp]
    (cos_ref, sin_ref, qn_ref, qr_ref, latn_ref, krn_ref, cosn_ref, sinn_ref,
     gkn_ref, gkr_ref, wuk_ref, wukt_ref, wuv_ref, o_ref, m_scr, l_scr, acc_scr, lhs_scr) = refs[2 * pp:]
    g = pl.program_id(1)
    rows = n_q * MLA_HEADS
    half = ROPE_DIM // 2
    n_up = wukt_ref.shape[0]

    @pl.when(g == 0)
    def _():
        m_scr[...] = jnp.full_like(m_scr, NEG_INF)
        l_scr[...] = jnp.zeros_like(l_scr)
        acc_scr[...] = jnp.zeros_like(acc_scr)
        lhs_scr[:n_up, :] = wukt_ref[...]
        qn = (qn_ref[0].astype(F32) * gkn_ref[...]).astype(BF16)
        lhs_scr[n_up:, :] = _dot_nt(qn, wuk_ref[...]).astype(BF16)

    def scores(big, c16, krt, cos_t, sin_t):
        t = c16.shape[0]
        knt = big[:n_up]
        ssq_n = jnp.sum((knt * knt).reshape(MLA_HEADS, NOPE_DIM, t), axis=1)
        ssq_r = jnp.sum(krt * krt, axis=0, keepdims=True)
        r = lax.rsqrt((ssq_n + ssq_r) * (1.0 / QK_DIM) + NORM_EPS)
        kg = krt * gkr_ref[...]
        x1, x2 = kg[:half], kg[half:]
        k_rot = jnp.concatenate([x1 * cos_t - x2 * sin_t, x2 * cos_t + x1 * sin_t], axis=0).astype(BF16)
        return (big[n_up:] + _dot(qr_ref[0], k_rot)) * jnp.concatenate([r] * n_q, axis=0)

    def absorb(parts):
        local = []
        for s, c16 in parts:
            m_c = jnp.max(s, axis=-1, keepdims=True)
            p = jnp.exp2(s - m_c)
            local.append((m_c, jnp.sum(p, axis=-1, keepdims=True), p.astype(BF16), c16))
        accs = [_dot(p16, c16) for _, _, p16, c16 in local]
        m_old = m_scr[...]
        m_new = functools.reduce(jnp.maximum, [m_c for m_c, _, _, _ in local], m_old)
        w_old = jnp.exp2(m_old - m_new)
        l = l_scr[...] * w_old
        acc = acc_scr[...] * w_old
        for (m_c, l_c, _, _), acc_c in zip(local, accs):
            w = jnp.exp2(m_c - m_new)
            l = l + l_c * w
            acc = acc + acc_c * w
        m_scr[...] = m_new
        l_scr[...] = l
        acc_scr[...] = acc

    n_sub = 4 if pp % 4 == 0 else 1
    per = pp // n_sub
    span = per * PAGE_SIZE
    lat_parts = [jnp.concatenate([lat_refs[i][0].astype(BF16) for i in range(c * per, (c + 1) * per)], axis=0)
                 for c in range(n_sub)]
    bigs = [_dot_nt(lhs_scr[...], lp) for lp in lat_parts]
    absorb([(scores(bigs[c], lat_parts[c],
                    jnp.concatenate([kr_refs[i][0] for i in range(c * per, (c + 1) * per)], axis=1),
                    cos_ref[0, :, c * span:(c + 1) * span], sin_ref[0, :, c * span:(c + 1) * span]), lat_parts[c])
            for c in range(n_sub)])

    @pl.when(g == pl.num_programs(1) - 1)
    def _():
        c_new = latn_ref[0].astype(BF16)
        s_new = scores(_dot_nt(lhs_scr[...], c_new), c_new, krn_ref[0], cosn_ref[...], sinn_ref[...])
        absorb([(jnp.where(_new_token_mask(rows, n_q), s_new, NEG_INF), c_new)])
        o_lat = (acc_scr[...] * (1.0 / l_scr[...])).astype(BF16)
        o_ref[0] = _fold_heads(_dot(o_lat, wuv_ref[...]), n_q, V_DIM).astype(o_ref.dtype)


def _mla_decode(page_table, cache_lat, cache_kr, cos_t, sin_t, qn_bd, qr, lat_new, kr_new, cos_new, sin_new,
                gkn, gkr, wuk, wukt, wuv, pp):
    bd, n_pages = page_table.shape
    n_q = qr.shape[1] // MLA_HEADS
    rows = qr.shape[1]
    c_kv = cache_lat.shape[-1]

    def page_spec(shape, i):
        return pl.BlockSpec((1,) + shape, lambda b, g, pt: (pt[b, g * pp + i], 0, 0))

    def batch_spec(shape):
        return pl.BlockSpec((1,) + shape, lambda b, g, pt: (b, 0, 0))

    def const_spec(shape):
        n = len(shape)
        return pl.BlockSpec(shape, lambda b, g, pt: (0,) * n)

    in_specs = [page_spec((PAGE_SIZE, c_kv), i) for i in range(pp)]
    in_specs += [page_spec((ROPE_DIM, PAGE_SIZE), i) for i in range(pp)]
    in_specs += [
        pl.BlockSpec((1, ROPE_DIM // 2, pp * PAGE_SIZE), lambda b, g, pt: (g, 0, 0)),
        pl.BlockSpec((1, ROPE_DIM // 2, pp * PAGE_SIZE), lambda b, g, pt: (g, 0, 0)),
        batch_spec(qn_bd.shape[1:]), batch_spec(qr.shape[1:]),
        batch_spec(lat_new.shape[1:]), batch_spec(kr_new.shape[1:]),
        const_spec(cos_new.shape), const_spec(sin_new.shape),
        const_spec(gkn.shape), const_spec(gkr.shape), const_spec(wuk.shape), const_spec(wukt.shape),
        const_spec(wuv.shape),
    ]
    grid_spec = pltpu.PrefetchScalarGridSpec(
        num_scalar_prefetch=1,
        grid=(bd, n_pages // pp),
        in_specs=in_specs,
        out_specs=pl.BlockSpec((1, n_q, wuv.shape[1]), lambda b, g, pt: (b, 0, 0)),
        scratch_shapes=[pltpu.VMEM((rows, 1), F32), pltpu.VMEM((rows, 1), F32),
                        pltpu.VMEM((rows, c_kv), F32), pltpu.VMEM((wukt.shape[0] + rows, c_kv), BF16)],
    )
    return pl.pallas_call(
        functools.partial(_mla_dec_body, pp=pp, n_q=n_q),
        grid_spec=grid_spec,
        out_shape=jax.ShapeDtypeStruct((bd, n_q, wuv.shape[1]), BF16),
        compiler_params=pltpu.CompilerParams(dimension_semantics=("arbitrary", "arbitrary"),
                                             vmem_limit_bytes=VMEM_LIMIT),
        name="mla_decode",
    )(page_table, *([cache_lat] * pp), *([cache_kr] * pp), cos_t, sin_t, qn_bd, qr, lat_new, kr_new,
      cos_new, sin_new, gkn, gkr, wuk, wukt, wuv)


def _fox_dec_body(pt_ref, *refs, pp, n_q):
    k_refs = refs[:pp]
    v_refs = refs[pp:2 * pp]
    lf_refs = refs[2 * pp:3 * pp]
    us_ref, q_ref, kn_ref, vn_ref, sufn_ref, o_ref, m_scr, l_scr, acc_scr, suf_scr = refs[3 * pp:]
    g = pl.program_id(1)
    rows = n_q * FOX_HEADS

    def chunk(kt16, vt16, bias, mask):
        s = _dot(q_ref[0], kt16) + jnp.concatenate([bias] * n_q, axis=0)
        if mask is not None:
            s = jnp.where(mask, s, NEG_INF)
        p, alpha = _softmax_step(s, m_scr, l_scr)
        acc_scr[...] = acc_scr[...] * alpha + _dot_nt(p.astype(BF16), vt16)

    @pl.when(g == 0)
    def _():
        m_scr[...] = jnp.full_like(m_scr, NEG_INF)
        l_scr[...] = jnp.zeros_like(l_scr)
        acc_scr[...] = jnp.zeros_like(acc_scr)
        suf_scr[...] = jnp.zeros_like(suf_scr)
        chunk(kn_ref[0].astype(BF16), vn_ref[0].astype(BF16), sufn_ref[0], _new_token_mask(rows, n_q))

    lfs = [lf_refs[i][0] for i in range(pp)]
    pieces = []
    for lf in lfs:
        hi = lf.astype(BF16).astype(F32)
        mid = (lf - hi).astype(BF16).astype(F32)
        pieces += [hi, mid, lf - hi - mid]
    w = _dot(jnp.concatenate(pieces, axis=0).astype(BF16), us_ref[...])
    carry = suf_scr[...]
    biases = [None] * pp
    for i in reversed(range(pp)):
        base = 3 * FOX_HEADS * i
        within = (w[base:base + FOX_HEADS] + w[base + FOX_HEADS:base + 2 * FOX_HEADS]
                  + w[base + 2 * FOX_HEADS:base + 3 * FOX_HEADS])
        biases[i] = (within + carry) * LOG2E
        carry = carry + jnp.sum(lfs[i], axis=1, keepdims=True)
    suf_scr[...] = carry
    chunk(jnp.concatenate([k_refs[i][0].astype(BF16) for i in range(pp)], axis=1),
          jnp.concatenate([v_refs[i][0].astype(BF16) for i in range(pp)], axis=1),
          jnp.concatenate(biases, axis=1), None)

    @pl.when(g == pl.num_programs(1) - 1)
    def _():
        out = acc_scr[...] * (1.0 / l_scr[...])
        o_ref[0] = _fold_heads(out, n_q, FOX_DIM).astype(o_ref.dtype)


def _fox_decode(page_table, cache_kt, cache_vt, cache_lf, q_bd, kt_new, vt_new, suf_new, pp):
    bd, n_pages = page_table.shape
    rows = q_bd.shape[1]
    n_q = rows // FOX_HEADS
    fw = cache_kt.shape[1]
    ng = n_pages // pp
    i_tok = np.arange(PAGE_SIZE)
    u_strict = jnp.asarray(i_tok[:, None] > i_tok[None, :], dtype=BF16)

    def page_spec(shape, i):
        return pl.BlockSpec((1,) + shape, lambda b, g, pt: (pt[b, (ng - 1 - g) * pp + i], 0, 0))

    def batch_spec(shape):
        return pl.BlockSpec((1,) + shape, lambda b, g, pt: (b,) + (0,) * len(shape))

    in_specs = [page_spec((fw, PAGE_SIZE), i) for i in range(pp)]
    in_specs += [page_spec((fw, PAGE_SIZE), i) for i in range(pp)]
    in_specs += [page_spec((FOX_HEADS, PAGE_SIZE), i) for i in range(pp)]
    in_specs += [
        pl.BlockSpec(u_strict.shape, lambda b, g, pt: (0, 0)),
        batch_spec(q_bd.shape[1:]), batch_spec(kt_new.shape[1:]), batch_spec(vt_new.shape[1:]),
        batch_spec(suf_new.shape[1:]),
    ]
    grid_spec = pltpu.PrefetchScalarGridSpec(
        num_scalar_prefetch=1,
        grid=(bd, ng),
        in_specs=in_specs,
        out_specs=pl.BlockSpec((1, n_q, fw), lambda b, g, pt: (b, 0, 0)),
        scratch_shapes=[pltpu.VMEM((rows, 1), F32), pltpu.VMEM((rows, 1), F32), pltpu.VMEM((rows, fw), F32),
                        pltpu.VMEM((FOX_HEADS, PAGE_SIZE), F32)],
    )
    return pl.pallas_call(
        functools.partial(_fox_dec_body, pp=pp, n_q=n_q),
        grid_spec=grid_spec,
        out_shape=jax.ShapeDtypeStruct((bd, n_q, fw), BF16),
        compiler_params=pltpu.CompilerParams(dimension_semantics=("arbitrary", "arbitrary"),
                                             vmem_limit_bytes=VMEM_LIMIT),
        name="fox_decode",
    )(page_table, *([cache_kt] * pp), *([cache_vt] * pp), *([cache_lf] * pp), u_strict, q_bd, kt_new, vt_new,
      suf_new)


ROUTE_IDX, ROUTE_RANK, ROUTE_GATE = 0, TOP_K, 2 * TOP_K


def _lane_pick(x, lane, which):
    return jnp.sum(jnp.where(lane == which, x, 0.0), axis=-1, keepdims=True)


def _merge_body(*refs, n_experts, n_aliased):
    (h_ref, oa_ref, ob_ref, sga_ref, sgb_ref, tri_ref, wba_ref, wbb_ref, wo_ref, fn_ref, wr_ref, br_ref,
     h2_o, xn_o, route_o, cnt_o) = refs[n_aliased:]
    tm = h_ref.shape[0]
    ua = _dot(oa_ref[...], wba_ref[...]) * sga_ref[...].astype(F32)
    ub = _dot(ob_ref[...], wbb_ref[...]) * sgb_ref[...].astype(F32)
    h2 = h_ref[...] + _dot((ua + ub).astype(BF16), wo_ref[...])
    h2_o[...] = h2
    xn = h2 * lax.rsqrt(jnp.mean(h2 * h2, axis=-1, keepdims=True) + NORM_EPS) * fn_ref[...]
    xn_o[...] = xn.astype(BF16)
    x_hi = xn.astype(BF16)
    x_lo = (xn - x_hi.astype(F32)).astype(BF16)
    logits = _dot(jnp.concatenate([x_hi, x_lo, x_hi], axis=1), wr_ref[...]) + br_ref[...]
    lane = lax.broadcasted_iota(jnp.int32, (tm, LANES), 1).astype(F32)
    logits = jnp.where(lane < n_experts, logits, NEG_INF)
    route = jnp.zeros((tm, LANES), F32)
    vals = jnp.full((tm, LANES), NEG_INF, F32)
    onehot = jnp.zeros((tm, LANES), F32)
    args = []
    for k in range(TOP_K):
        top = jnp.max(logits, axis=-1, keepdims=True)
        arg = jnp.min(jnp.where(logits == top, lane, float(LANES)), axis=-1, keepdims=True)
        route = jnp.where(lane == ROUTE_IDX + k, arg, route)
        vals = jnp.where(lane == k, top, vals)
        onehot = onehot + jnp.where(lane == arg, 1.0, 0.0)
        logits = jnp.where(lane == arg, NEG_INF, logits)
        args.append(arg)
    e = jnp.exp(vals - jnp.max(vals, axis=-1, keepdims=True))
    gates = e / jnp.sum(e, axis=-1, keepdims=True)
    earlier = _dot(tri_ref[...], onehot.astype(BF16))
    for k in range(TOP_K):
        route = jnp.where(lane == ROUTE_RANK + k, _lane_pick(earlier, lane, args[k]), route)
    in_gate = (lane >= ROUTE_GATE) & (lane < ROUTE_GATE + TOP_K)
    route_o[...] = jnp.where(in_gate, pltpu.roll(gates, ROUTE_GATE, 1), route)
    cnt_o[0] = jnp.broadcast_to(jnp.sum(onehot, axis=0, keepdims=True), cnt_o.shape[1:])


def _merge(h, oa, ob, sga, sgb, mw, tm, in_offset, n_tiles, n_rows_total, tile_offset, prev):
    d = h.shape[1]
    n_experts = mw["n_experts"]
    n_tiles_total = n_rows_total // tm
    i = np.arange(tm)
    tri_strict = jnp.asarray(i[None, :] < i[:, None], dtype=BF16)

    def row_spec(width):
        return pl.BlockSpec((tm, width), lambda i: (i + in_offset, 0))

    def out_spec(width):
        return pl.BlockSpec((tm, width), lambda i: (i + tile_offset, 0))

    weights = [mw[k] for k in ("wba", "wbb", "wo", "fn", "wr", "br")]
    n_aliased = 0 if prev is None else len(prev)
    aliased = [] if prev is None else list(prev)
    return pl.pallas_call(
        functools.partial(_merge_body, n_experts=n_experts, n_aliased=n_aliased),
        grid=(n_tiles,),
        in_specs=[pl.BlockSpec(memory_space=pl.ANY)] * n_aliased
        + [row_spec(d), row_spec(oa.shape[1]), row_spec(ob.shape[1]), row_spec(d), row_spec(d),
           _const_spec(tri_strict.shape)] + [_const_spec(w.shape) for w in weights],
        out_specs=[out_spec(d), out_spec(d), out_spec(LANES),
                   pl.BlockSpec((1, 8, LANES), lambda i: (i + tile_offset, 0, 0))],
        out_shape=[jax.ShapeDtypeStruct((n_rows_total, d), F32), jax.ShapeDtypeStruct((n_rows_total, d), BF16),
                   jax.ShapeDtypeStruct((n_rows_total, LANES), F32),
                   jax.ShapeDtypeStruct((n_tiles_total, 8, LANES), F32)],
        input_output_aliases={k: k for k in range(n_aliased)},
        compiler_params=pltpu.CompilerParams(dimension_semantics=("arbitrary",), vmem_limit_bytes=VMEM_LIMIT),
        name="merge",
    )(*aliased, h, oa, ob, sga, sgb, tri_strict, *weights)


def _dest_body(route_ref, base_ref, dest_o):
    tm = route_ref.shape[0]
    lane = lax.broadcasted_iota(jnp.int32, (tm, LANES), 1).astype(F32)
    route = route_ref[...]
    base = base_ref[0, 0:1, :]
    out = jnp.zeros((tm, LANES), F32)
    for k in range(TOP_K):
        e_k = _lane_pick(route, lane, float(ROUTE_IDX + k))
        row = _lane_pick(base, lane, e_k) + _lane_pick(route, lane, float(ROUTE_RANK + k))
        out = jnp.where(lane == k, row, out)
    dest_o[...] = out.astype(jnp.int32)


def _dest(route, base, tm):
    rows = route.shape[0]
    return pl.pallas_call(
        _dest_body,
        grid=(rows // tm,),
        in_specs=[pl.BlockSpec((tm, LANES), lambda i: (i, 0)), pl.BlockSpec((1, 8, LANES), lambda i: (i, 0, 0))],
        out_specs=pl.BlockSpec((tm, LANES), lambda i: (i, 0)),
        out_shape=jax.ShapeDtypeStruct((rows, LANES), jnp.int32),
        compiler_params=pltpu.CompilerParams(dimension_semantics=("arbitrary",)),
        name="dest",
    )(route, base)


def _combine_body(*refs, n_aliased):
    refs = refs[n_aliased:]
    g_refs = refs[:TOP_K]
    route_ref, h2_ref, y_o = refs[TOP_K:]
    tm = h2_ref.shape[0]
    lane = lax.broadcasted_iota(jnp.int32, (tm, LANES), 1).astype(F32)
    route = route_ref[...]
    y = h2_ref[...]
    for k in range(TOP_K):
        y = y + _lane_pick(route, lane, float(ROUTE_GATE + k)) * g_refs[k][...].astype(F32)
    y_o[...] = y


def _combine(g, route, h2, tm, tile_offset, n_tiles, out_tiles_total, out_offset, prev):
    n_tok, d = h2.shape
    tiles_total = n_tok // tm
    spec = lambda width: pl.BlockSpec((tm, width), lambda i: (i + tile_offset, 0))
    g_specs = [pl.BlockSpec((tm, d), functools.partial(lambda i, k: (i + tile_offset + k * tiles_total, 0), k=k))
               for k in range(TOP_K)]
    n_aliased = 0 if prev is None else 1
    return pl.pallas_call(
        functools.partial(_combine_body, n_aliased=n_aliased),
        grid=(n_tiles,),
        in_specs=[pl.BlockSpec(memory_space=pl.ANY)] * n_aliased + g_specs + [spec(LANES), spec(d)],
        out_specs=pl.BlockSpec((tm, d), lambda i: (i + out_offset, 0)),
        out_shape=jax.ShapeDtypeStruct((out_tiles_total * tm, d), F32),
        input_output_aliases={0: 0} if prev is not None else {},
        compiler_params=pltpu.CompilerParams(dimension_semantics=("arbitrary",), vmem_limit_bytes=VMEM_LIMIT),
        name="combine",
    )(*([prev] if prev is not None else []), *([g] * TOP_K), route, h2)


def _ffn_body(be_ref, nu_ref, x_ref, wg_ref, bg_ref, wu_ref, bu_ref, wd_ref, bd_ref, o_ref, wg16, wu16, wd16):
    i = pl.program_id(0)

    @pl.when(i < nu_ref[0])
    def _():
        @pl.when((i == 0) | (be_ref[i] != be_ref[jnp.maximum(i - 1, 0)]))
        def _():
            wg16[...] = wg_ref[0].astype(BF16)
            wu16[...] = wu_ref[0].astype(BF16)
            wd16[...] = wd_ref[0].astype(BF16)

        x = x_ref[...]
        g = _dot(x, wg16[...]) + bg_ref[0]
        u = _dot(x, wu16[...]) + bu_ref[0]
        g = jnp.minimum(g, SWIGLU_LIMIT)
        u = jnp.clip(u, -SWIGLU_LIMIT, SWIGLU_LIMIT)
        hid = ((u + 1.0) * g * jax.nn.sigmoid(SWIGLU_ALPHA * g)).astype(BF16)
        o_ref[...] = (_dot(hid, wd16[...]) + bd_ref[0]).astype(o_ref.dtype)


def _ffn(block_e, n_used, x_rows, w_gate, b_gate, w_up, b_up, w_down, b_down, tb):
    n_rows, d = x_rows.shape
    e, _, f = w_gate.shape

    def wspec(shape):
        return pl.BlockSpec((1,) + shape, lambda i, be, nu: (be[i], 0, 0))

    grid_spec = pltpu.PrefetchScalarGridSpec(
        num_scalar_prefetch=2,
        grid=(n_rows // tb,),
        in_specs=[pl.BlockSpec((tb, d), lambda i, be, nu: (i, 0)),
                  wspec((d, f)), wspec((1, f)), wspec((d, f)), wspec((1, f)), wspec((f, d)), wspec((1, d))],
        out_specs=pl.BlockSpec((tb, d), lambda i, be, nu: (i, 0)),
        scratch_shapes=[pltpu.VMEM((d, f), BF16), pltpu.VMEM((d, f), BF16), pltpu.VMEM((f, d), BF16)],
    )
    return pl.pallas_call(
        _ffn_body,
        grid_spec=grid_spec,
        out_shape=jax.ShapeDtypeStruct((n_rows, d), BF16),
        compiler_params=pltpu.CompilerParams(dimension_semantics=("arbitrary",), vmem_limit_bytes=VMEM_LIMIT),
        name="ffn",
    )(block_e, n_used, x_rows, w_gate, b_gate.reshape(e, 1, f), w_up, b_up.reshape(e, 1, f),
      w_down, b_down.reshape(e, 1, d))


def _rope_tables(pos):
    half = ROPE_DIM // 2
    inv_freq = ROPE_THETA ** (-jnp.arange(half, dtype=F32) / half)
    ang = pos.astype(F32)[:, None] * inv_freq[None, :]
    cos, sin = jnp.cos(ang), jnp.sin(ang)
    n = pos.shape[0]
    zeros = jnp.zeros((n, half), F32)
    c = jnp.concatenate([jnp.ones((n, NOPE_DIM), F32), cos, cos, jnp.ones((n, LANES - QK_DIM), F32)], axis=1)
    s1 = jnp.concatenate([jnp.zeros((n, NOPE_DIM), F32), -sin, zeros, jnp.zeros((n, LANES - QK_DIM), F32)], axis=1)
    s2 = jnp.concatenate([jnp.zeros((n, NOPE_DIM), F32), zeros, sin, jnp.zeros((n, LANES - QK_DIM), F32)], axis=1)
    return c, s1, s2


def _pad_cols(w, width, offset=0):
    return jnp.pad(w, ((0, 0), (offset, width - offset - w.shape[1])))


def _proj_weights(attn_norm, w_in, b_forget, g_qlat, w_uq, g_kvlat, w_ukv, g_mla_q, g_mla_k, g_fox_q, g_fox_k):
    q_lora, c_kv = w_uq.shape[0], w_ukv.shape[0]
    fw = FOX_HEADS * FOX_DIM
    d = w_in.shape[0]
    sizes = (q_lora, c_kv, ROPE_DIM, fw, fw, fw, FOX_HEADS, d, d)
    offs = np.concatenate([[0], np.cumsum(sizes)])
    wq, wkv, wkr, wfq, wfk, wfv, wf, wga, wgb = [w_in[:, offs[i]:offs[i + 1]] for i in range(len(sizes))]
    pad_head = LANES - QK_DIM
    wuq = jnp.pad(w_uq, ((0, 0), (0, 0), (0, pad_head))).reshape(q_lora, MLA_HEADS * HEAD_SLOT)
    wuk = jnp.pad(w_ukv[:, :, :NOPE_DIM], ((0, 0), (0, 0), (0, LANES - NOPE_DIM))).reshape(c_kv, MLA_HEADS * HEAD_SLOT)
    wuv = w_ukv[:, :, NOPE_DIM:].reshape(c_kv, MLA_HEADS * V_DIM)
    row = lambda v: v.reshape(1, -1).astype(F32)
    return {
        "an": row(attn_norm), "wq": wq.astype(BF16), "gql": row(g_qlat), "wuq": wuq.astype(BF16),
        "gq": _pad_cols(row(g_mla_q) * MLA_SCALE, LANES),
        "wkv": wkv.astype(BF16), "gkv": row(g_kvlat),
        "wkr0": _pad_cols(wkr, LANES).astype(BF16), "wkr64": _pad_cols(wkr, LANES, NOPE_DIM).astype(BF16),
        "wuk": wuk.astype(BF16), "gk": _pad_cols(row(g_mla_k), LANES), "wuv": wuv.astype(BF16),
        "wfq": wfq.astype(BF16), "gfq": jnp.tile(row(g_fox_q), (1, FOX_HEADS)) * FOX_SCALE,
        "wfk": wfk.astype(BF16), "gfk": jnp.tile(row(g_fox_k), (1, FOX_HEADS)),
        "wfv": wfv.astype(BF16),
        "wf3": _pad_cols(jnp.tile(wf, (1, 3)), LANES).astype(BF16),
        "bf3": _pad_cols(jnp.tile(row(b_forget), (1, 3)), LANES),
        "wga": wga.astype(BF16), "wgb": wgb.astype(BF16),
    }


def _tri(n, group=None):
    i = np.arange(n)
    m = i[None, :] <= i[:, None]
    if group is not None:
        m &= (i[None, :] // group) == (i[:, None] // group)
    return jnp.asarray(m, dtype=BF16)


def _pick_tile(n, target):
    t = min(n, target)
    while n % t:
        t //= 2
    return t


def _moe_rows(xn, route, cnt, w_gate, b_gate, w_up, b_up, w_down, b_down, tm, tb):
    n_tok, d = xn.shape
    n_experts = w_gate.shape[0]
    n_assign = n_tok * TOP_K
    n_tiles = n_tok // tm
    cnt = cnt[:, 0, :n_experts].astype(jnp.int32)
    counts = jnp.sum(cnt, axis=0)
    tile_off = jnp.cumsum(cnt, axis=0) - cnt
    padded = (counts + tb - 1) // tb * tb
    pad_end = jnp.cumsum(padded)
    pad_start = pad_end - padded
    start = jnp.cumsum(counts) - counts
    n_blocks = -(-n_assign // tb) + n_experts
    n_used = (pad_end[-1] // tb).astype(jnp.int32).reshape(1)
    block_start = jnp.arange(n_blocks, dtype=jnp.int32) * tb
    block_e = jnp.minimum(jnp.sum((pad_end[None, :] <= block_start[:, None]).astype(jnp.int32), axis=1),
                          n_experts - 1)
    base = jnp.zeros((n_tiles, 8, LANES), F32).at[:, 0, :n_experts].set((pad_start[None, :] + tile_off).astype(F32))
    dest = _dest(route, base, tm)[:, :TOP_K].T.reshape(-1)
    flat_e = route[:, ROUTE_IDX:ROUTE_IDX + TOP_K].astype(jnp.int32).reshape(-1)
    order = jnp.argsort(flat_e)
    delta = start[block_e] - pad_start[block_e]
    pos = jnp.arange(n_blocks * tb, dtype=jnp.int32).reshape(n_blocks, tb) + delta[:, None]
    src_tok = order[jnp.clip(pos, 0, n_assign - 1)] // TOP_K
    x_rows = xn[src_tok.reshape(-1)]
    y_rows = _ffn(block_e.astype(jnp.int32), n_used, x_rows, w_gate, b_gate, w_up, b_up, w_down, b_down, tb)
    return y_rows[dest]


def kernel(x_prompt, x_sample, cache_mla_latent, cache_mla_krope, cache_fox_k, cache_fox_v, cache_fox_logf,
           page_table, meta_tokens, attn_norm, w_in, b_forget, g_qlat, w_uq, g_kvlat, w_ukv, g_mla_q, g_mla_k,
           g_fox_q, g_fox_k, w_branch_mla, w_branch_fox, w_out, ffn_norm, w_router, b_router, w_gate, b_gate,
           w_up, b_up, w_down, b_down):
    assert attn_norm.shape[0] == 1, "single-layer trunk"
    b, seq, d = x_prompt.shape
    bd, n_q, _ = x_sample.shape
    n_pages = page_table.shape[1]
    n_experts = w_router.shape[-1]
    fw = FOX_HEADS * FOX_DIM
    c_kv = w_ukv.shape[1]

    pw = _proj_weights(attn_norm[0], w_in[0], b_forget[0], g_qlat[0], w_uq[0], g_kvlat[0], w_ukv[0],
                       g_mla_q[0], g_mla_k[0], g_fox_q[0], g_fox_k[0])

    tm = _pick_tile(seq, 512)
    nt = seq // tm
    main = _proj(x_prompt.reshape(b * seq, d), _rope_tables(N_META + jnp.arange(seq)), _tri(tm), pw, b, nt, tm)
    meta_x = jnp.pad(meta_tokens.astype(F32), ((0, LANES - N_META), (0, 0)))
    meta = _proj(meta_x, _rope_tables(jnp.arange(LANES)), _tri(LANES), pw, 1, 1, LANES)
    past = n_pages * PAGE_SIZE
    ns = bd * n_q
    samp = _proj(x_sample.reshape(ns, d), _rope_tables(past + jnp.arange(ns) % n_q), _tri(ns, n_q), pw, 1, 1, ns)

    (ckv_p, kr_p, kfox_p, vfox_p, logf_p, _, nf_p, qm_p, km_p, vme_p, vmo_p, qf_p, kf_p, vfe_p, vfo_p,
     sga_p, sgb_p) = main
    (ckv_m, kr_m, kfox_m, vfox_m, logf_m, _, nf_m, _, km_m, vme_m, vmo_m, _, kf_m, vfe_m, vfo_m, _, _) = meta
    (ckv_s, kr_s, kfox_s, vfox_s, logf_s, fcol_s, _, qm_s, _, _, _, qf_s, _, _, _, sga_s, sgb_s) = samp

    tb = _pick_tile(seq, 256)
    r3 = lambda a: a.reshape(b, seq, a.shape[-1])
    o_mla = _flash(r3(qm_p), r3(km_p), r3(vme_p), r3(vmo_p), km_m, vme_m, vmo_m, None, None, 1, tb)
    nf_meta = nf_m[0] - nf_m[0][:, N_META - 1:N_META]
    nf_blocks = jnp.swapaxes(nf_p.reshape(b, FOX_HEADS, seq // tb, tb), 1, 2)
    o_fox = _flash(r3(qf_p), r3(kf_p), r3(vfe_p), r3(vfo_p), kf_m, vfe_m, vfo_m, nf_blocks, nf_meta, 2, tb)

    pp = _pick_tile(n_pages, 16)
    pp_m = _pick_tile(n_pages, 32)
    rows = n_q * MLA_HEADS
    eye_h = jnp.eye(MLA_HEADS, dtype=BF16)
    q4 = qm_s.reshape(bd, n_q, MLA_HEADS, HEAD_SLOT)
    qn_bd = (q4[:, :, :, None, :NOPE_DIM] * eye_h[None, None, :, :, None]).reshape(bd, rows, MLA_HEADS * NOPE_DIM)
    qr = q4[..., NOPE_DIM:QK_DIM].reshape(bd, rows, ROPE_DIM)
    half = ROPE_DIM // 2
    inv_freq = ROPE_THETA ** (-jnp.arange(half, dtype=F32) / half)
    ang = jnp.arange(past, dtype=F32).reshape(n_pages // pp_m, 1, pp_m * PAGE_SIZE) * inv_freq[None, :, None]
    ang_new = (past + jnp.minimum(jnp.arange(PAGE_SIZE), n_q - 1)).astype(F32)[None, :] * inv_freq[:, None]
    pad_new = lambda a: jnp.pad(a.reshape(bd, n_q, a.shape[-1]), ((0, 0), (0, PAGE_SIZE - n_q), (0, 0)))
    pad_new_t = lambda a: jnp.pad(jnp.swapaxes(a.reshape(bd, n_q, a.shape[-1]), 1, 2),
                                  ((0, 0), (0, 0), (0, PAGE_SIZE - n_q)))
    w_ukv0 = w_ukv[0]
    wuk_c = w_ukv0[:, :, :NOPE_DIM].reshape(c_kv, MLA_HEADS * NOPE_DIM).astype(BF16)
    o_mla_s = _mla_decode(
        page_table, cache_mla_latent[0], jnp.swapaxes(cache_mla_krope[0], 1, 2), jnp.cos(ang), jnp.sin(ang),
        qn_bd, qr, pad_new(ckv_s), pad_new_t(kr_s), jnp.cos(ang_new), jnp.sin(ang_new),
        jnp.tile(g_mla_k[0][:NOPE_DIM].astype(F32), MLA_HEADS).reshape(1, -1),
        g_mla_k[0][NOPE_DIM:].astype(F32).reshape(ROPE_DIM, 1),
        wuk_c, wuk_c.T, pw["wuv"], pp_m)

    suf_new = jnp.pad(-LOG2E * jnp.swapaxes(fcol_s.reshape(bd, n_q, FOX_HEADS), 1, 2),
                      ((0, 0), (0, 0), (0, PAGE_SIZE - n_q)))
    qf4 = qf_s.reshape(bd, n_q, FOX_HEADS, 1, HEAD_SLOT)
    pair_of = (jnp.arange(FOX_HEADS)[:, None] // 2 == jnp.arange(FOX_HEADS // 2)[None, :]).astype(BF16)
    qf_bd = (qf4 * pair_of[None, None, :, :, None]).reshape(bd, rows, fw)
    dim_major = lambda c: jnp.transpose(c, (0, 2, 3, 1)).reshape(c.shape[0], fw, PAGE_SIZE)
    o_fox_s = _fox_decode(page_table, dim_major(cache_fox_k[0]), dim_major(cache_fox_v[0]),
                          jnp.swapaxes(cache_fox_logf[0], 1, 2), qf_bd,
                          pad_new_t(kfox_s), pad_new_t(vfox_s), suf_new, pp)

    wr = _pad_cols(w_router[0].astype(F32), LANES)
    wr_hi = wr.astype(BF16)
    wr_lo = (wr - wr_hi.astype(F32)).astype(BF16)
    mw = {
        "wba": w_branch_mla[0].astype(BF16), "wbb": w_branch_fox[0].astype(BF16), "wo": w_out[0].astype(BF16),
        "fn": ffn_norm[0].reshape(1, -1).astype(F32),
        "wr": jnp.concatenate([wr_hi, wr_hi, wr_lo], axis=0),
        "br": _pad_cols(b_router[0].reshape(1, -1).astype(F32), LANES),
        "n_experts": n_experts,
    }
    tmg = _pick_tile(ns, 512)
    assert (b * seq) % tmg == 0 and ns % tmg == 0
    tiles_p, tiles_s = b * seq // tmg, ns // tmg
    tiles_a = tiles_p // 2
    tiles_b = tiles_p - tiles_a + tiles_s
    flat = lambda a: a.reshape(-1, a.shape[-1])
    prompt_in = (flat(x_prompt), flat(o_mla), flat(o_fox), sga_p, sgb_p)
    sample_in = (flat(x_sample), flat(o_mla_s), flat(o_fox_s), sga_s, sgb_s)
    group_a = _merge(*prompt_in, mw, tmg, 0, tiles_a, tiles_a * tmg, 0, None)
    group_b = _merge(*prompt_in, mw, tmg, tiles_a, tiles_p - tiles_a, tiles_b * tmg, 0, None)
    group_b = _merge(*sample_in, mw, tmg, 0, tiles_s, tiles_b * tmg, tiles_p - tiles_a, group_b)

    def experts(group):
        h2, xn2, route, cnt = group
        g = _moe_rows(xn2, route, cnt, w_gate[0], b_gate[0], w_up[0], b_up[0], w_down[0], b_down[0], tmg,
                      _pick_tile(xn2.shape[0] * TOP_K, 512))
        return g, route, h2

    ga, gb = experts(group_a), experts(group_b)
    y_prompt = _combine(*ga, tmg, 0, tiles_a, tiles_p, 0, None)
    y_prompt = _combine(*gb, tmg, 0, tiles_p - tiles_a, tiles_p, tiles_a, y_prompt).reshape(b, seq, d)
    y_sample = _combine(*gb, tmg, tiles_p - tiles_a, tiles_s, tiles_s, 0, None).reshape(bd, n_q, d)

    def state_p(main_a, meta_a, tail):
        m = jnp.broadcast_to(meta_a[:N_META].reshape((1, N_META) + tail), (b, N_META) + tail)
        return jnp.concatenate([m, main_a.reshape((b, seq) + tail)], axis=1)[None]

    def state_s(a, tail):
        return a.reshape((1, bd, n_q) + tail)

    return (y_prompt, y_sample,
            state_p(ckv_p, ckv_m, (c_kv,)), state_p(kr_p, kr_m, (ROPE_DIM,)),
            state_p(kfox_p, kfox_m, (FOX_HEADS, FOX_DIM)), state_p(vfox_p, vfox_m, (FOX_HEADS, FOX_DIM)),
            state_p(logf_p, logf_m, (FOX_HEADS,)),
            state_s(ckv_s, (c_kv,)), state_s(kr_s, (ROPE_DIM,)),
            state_s(kfox_s, (FOX_HEADS, FOX_DIM)), state_s(vfox_s, (FOX_HEADS, FOX_DIM)),
            state_s(logf_s, (FOX_HEADS,)))
```
